```python
import jax
import jax.numpy as jnp
from jax import lax
import numpy as np

D_MODEL = 1024
BATCH = 2
SEQ = 8192
DEPTH = 2

CHUNK = 64
D_FF = 2816
D_CONV = 512
CONV_WIDTH = 31
GLA_HEADS = 4
GLA_DK = 64
GLA_DV = 128
D_GLA = GLA_HEADS * GLA_DV
D_MIX = D_CONV + D_GLA
GATE_RANK = 16
GATE_TAU = 16.0
N_MOD = 9
EPS = 1e-6
SPLITS = (D_CONV, 2 * D_CONV, 2 * D_CONV + GLA_HEADS * GLA_DK, 2 * D_CONV + 2 * GLA_HEADS * GLA_DK, 2 * D_CONV + 2 * GLA_HEADS * GLA_DK + D_GLA, 2 * D_CONV + 2 * GLA_HEADS * GLA_DK + 2 * D_GLA)
D_IN = SPLITS[-1] + GATE_RANK

kernel_name = 'hybrid_conformer_gla_macaron'


def rmsnorm(x, g):
    xf = x.astype(jnp.float32)
    y = xf * lax.rsqrt(jnp.mean(xf * xf, axis=-1, keepdims=True) + EPS)
    return (y * g.astype(jnp.float32)).astype(x.dtype)


def layernorm(x, g, b):
    xf = x.astype(jnp.float32)
    xc = xf - jnp.mean(xf, axis=-1, keepdims=True)
    y = xc * lax.rsqrt(jnp.mean(xc * xc, axis=-1, keepdims=True) + EPS)
    return (y * g.astype(jnp.float32) + b.astype(jnp.float32)).astype(x.dtype)


def modulate(h, shift, scale):
    return h * (1 + scale) + shift


def swiglu(h, w_in, w_out):
    gate, up = jnp.split(h @ w_in, 2, axis=-1)
    return (jax.nn.silu(gate) * up) @ w_out


def conformer_conv(a, b, w_dw, b_dw, g_ln, b_ln):
    u = a * jax.nn.sigmoid(b)
    u = jnp.pad(u, ((0, 0), (CONV_WIDTH - 1, 0), (0, 0)))
    y = lax.conv_general_dilated(u, w_dw.astype(u.dtype)[:, None, :], window_strides=(1,), padding='VALID', dimension_numbers=('NWC', 'WIO', 'NWC'), feature_group_count=D_CONV)
    return jax.nn.silu(layernorm(y + b_dw, g_ln, b_ln))


def gla(q, k, v, r, glr, w_gate_up, b_gate, g_norm):
    bsz, seq, _ = q.shape
    n = seq // CHUNK
    f32 = jnp.float32

    def heads(t, d):
        return t.astype(f32).reshape(bsz, n, CHUNK, GLA_HEADS, d).transpose(0, 3, 1, 2, 4)

    log_a = jax.nn.log_sigmoid((glr @ w_gate_up + b_gate).astype(f32)) / GATE_TAU
    qh = heads(q, GLA_DK) * (GLA_DK ** -0.5)
    kh = heads(k, GLA_DK)
    vh = heads(v, GLA_DV)
    bc = jnp.cumsum(heads(log_a, GLA_DK), axis=3)
    b_end = bc[:, :, :, -1:, :]
    q_fwd = qh * jnp.exp(bc)
    att_fwd = jnp.einsum('bhnik,bhnjk->bhnij', q_fwd, kh * jnp.exp(-bc))
    att_bwd = jnp.einsum('bhnik,bhnjk->bhnij', qh * jnp.exp(-bc), kh * jnp.exp(bc))
    tri = jnp.tril(jnp.ones((CHUNK, CHUNK), dtype=bool))
    o = jnp.einsum('bhnij,bhnjv->bhniv', jnp.where(tri, att_fwd, att_bwd), vh)
    u = jnp.einsum('bhnjk,bhnjv->bhnkv', kh * jnp.exp(b_end - bc), vh)
    g = jnp.exp(b_end[:, :, :, 0, :])

    def step(state, inp):
        g_c, u_c = inp
        return g_c[..., None] * state + u_c, state

    s0 = jnp.zeros((bsz, GLA_HEADS, GLA_DK, GLA_DV), f32)
    _, s_prev = lax.scan(step, s0, (jnp.moveaxis(g, 2, 0), jnp.moveaxis(u, 2, 0)))
    o = o + jnp.einsum('bhnik,nbhkv->bhniv', q_fwd, s_prev)
    o = o * lax.rsqrt(jnp.mean(o * o, axis=-1, keepdims=True) + EPS) * g_norm.astype(f32)[:, None, None, :]
    o = o.transpose(0, 2, 3, 1, 4).reshape(bsz, seq, D_GLA)
    return (o * jax.nn.silu(r.astype(f32))).astype(q.dtype)


def setup_inputs(seed: int = 0) -> dict:
    key = jax.random.key(seed)
    ks = jax.random.split(key, 26)
    L = DEPTH

    def nrm(k, shape, scale):
        return jax.random.normal(k, shape, jnp.float32) * scale

    return {
        'x': nrm(ks[0], (BATCH, SEQ, D_MODEL), 1.0),
        'c': nrm(ks[1], (BATCH, D_MODEL), 1.0),
        'w_ada': nrm(ks[2], (L, D_MODEL, N_MOD * D_MODEL), 0.5 * D_MODEL ** -0.5),
        'b_ada': nrm(ks[3], (L, N_MOD * D_MODEL), 0.02),
        'g_norm_ffn1': 1.0 + nrm(ks[4], (L, D_MODEL), 0.02),
        'w_ffn1_in': nrm(ks[5], (L, D_MODEL, 2 * D_FF), D_MODEL ** -0.5),
        'w_ffn1_out': nrm(ks[6], (L, D_FF, D_MODEL), D_FF ** -0.5),
        'g_norm_mix': 1.0 + nrm(ks[7], (L, D_MODEL), 0.02),
        'w_in': nrm(ks[8], (L, D_MODEL, D_IN), D_MODEL ** -0.5),
        'w_dw': nrm(ks[9], (L, CONV_WIDTH, D_CONV), CONV_WIDTH ** -0.5),
        'b_dw': nrm(ks[10], (L, D_CONV), 0.02),
        'g_conv_ln': 1.0 + nrm(ks[11], (L, D_CONV), 0.02),
        'b_conv_ln': nrm(ks[12], (L, D_CONV), 0.02),
        'w_gate_up': nrm(ks[13], (L, GATE_RANK, GLA_HEADS * GLA_DK), GATE_RANK ** -0.5),
        'b_gate': nrm(ks[14], (L, GLA_HEADS * GLA_DK), 0.02),
        'g_gla_norm': 1.0 + nrm(ks[15], (L, GLA_HEADS, GLA_DV), 0.02),
        'w_out': nrm(ks[16], (L, D_MIX, D_MODEL), D_MIX ** -0.5),
        'g_norm_ffn2': 1.0 + nrm(ks[17], (L, D_MODEL), 0.02),
        'w_ffn2_in': nrm(ks[18], (L, D_MODEL, 2 * D_FF), D_MODEL ** -0.5),
        'w_ffn2_out': nrm(ks[19], (L, D_FF, D_MODEL), D_FF ** -0.5),
        'g_norm_final': 1.0 + nrm(ks[20], (D_MODEL,), 0.02),
        'w_ada_final': nrm(ks[21], (D_MODEL, 2 * D_MODEL), 0.5 * D_MODEL ** -0.5),
        'b_ada_final': nrm(ks[22], (2 * D_MODEL,), 0.02),
    }


def reference(x, c, w_ada, b_ada, g_norm_ffn1, w_ffn1_in, w_ffn1_out, g_norm_mix, w_in, w_dw, b_dw, g_conv_ln, b_conv_ln, w_gate_up, b_gate, g_gla_norm, w_out, g_norm_ffn2, w_ffn2_in, w_ffn2_out, g_norm_final, w_ada_final, b_ada_final):
    bsz = x.shape[0]
    c_act = jax.nn.silu(c)
    for l in range(DEPTH):
        mod = (c_act @ w_ada[l] + b_ada[l]).reshape(bsz, N_MOD, 1, D_MODEL)
        h = modulate(rmsnorm(x, g_norm_ffn1[l]), mod[:, 0], mod[:, 1])
        x = x + 0.5 * mod[:, 2] * swiglu(h, w_ffn1_in[l], w_ffn1_out[l])
        h = modulate(rmsnorm(x, g_norm_mix[l]), mod[:, 3], mod[:, 4])
        a, b, q, k, v, r, glr = jnp.split(h @ w_in[l], SPLITS, axis=-1)
        y_conv = conformer_conv(a, b, w_dw[l], b_dw[l], g_conv_ln[l], b_conv_ln[l])
        y_gla = gla(q, k, v, r, glr, w_gate_up[l], b_gate[l], g_gla_norm[l])
        x = x + mod[:, 5] * (jnp.concatenate([y_conv, y_gla], axis=-1) @ w_out[l])
        h = modulate(rmsnorm(x, g_norm_ffn2[l]), mod[:, 6], mod[:, 7])
        x = x + 0.5 * mod[:, 8] * swiglu(h, w_ffn2_in[l], w_ffn2_out[l])
    fmod = (c_act @ w_ada_final + b_ada_final).reshape(bsz, 2, 1, D_MODEL)
    return modulate(rmsnorm(x, g_norm_final), fmod[:, 0], fmod[:, 1])
```

```python
import functools

import jax
import jax.numpy as jnp
from jax import lax
from jax.experimental import pallas as pl
from jax.experimental.pallas import tpu as pltpu

D_MODEL = 1024
D_FF = 2816
D_CONV = 512
CONV_WIDTH = 31
GLA_HEADS = 4
GLA_DK = 64
GLA_DV = 128
D_QK = GLA_HEADS * GLA_DK
D_GLA = GLA_HEADS * GLA_DV
D_MIX = D_CONV + D_GLA
GATE_RANK = 16
GATE_TAU = 16.0
CHUNK = 64
N_MOD = 9
EPS = 1e-6
D_MAIN = 2 * D_CONV + 2 * D_QK + 2 * D_GLA

LANES = 128
SUBLANES = 8
VMEM_LIMIT_BYTES = 56 * 1024 * 1024

FFN_TILE = 512
MIX_TILE = 256
GLA_TILE = 2 * CHUNK
CONV_ROWS = 32
CONV_PAD = 32

F32 = jnp.float32
BF16 = jnp.bfloat16


def _dot(a, b):
    return jnp.dot(a, b, preferred_element_type=F32)


def _dot_nt(a, b):
    return lax.dot_general(a, b, (((1,), (1,)), ((), ())), preferred_element_type=F32)


def _dot_tn(a, b):
    return lax.dot_general(a, b, (((0,), (0,)), ((), ())), preferred_element_type=F32)


def _sigmoid(x):
    return 1.0 / (1.0 + jnp.exp(-x))


def _silu(x):
    return x * _sigmoid(x)


def _log_sigmoid(x):
    return jnp.minimum(x, 0.0) - jnp.log(1.0 + jnp.exp(-jnp.abs(x)))


def _rms_mod(x, g, shift, scale):
    y = x * lax.rsqrt(jnp.mean(x * x, axis=-1, keepdims=True) + EPS)
    return (y * g) * (1.0 + scale) + shift


def _split_bf16(x, parts):
    out = []
    for _ in range(parts):
        p = x.astype(BF16)
        out.append(p)
        x = x - p.astype(F32)
    return out


def _ada_kernel(c_ref, w_ref, b_ref, o_ref):
    c = _silu(c_ref[...])
    o_ref[...] = _dot(c.astype(BF16), w_ref[...].astype(BF16)) + b_ref[...]


def _ada_call(c_pad, w, b, tn):
    n_layers, d, n = w.shape
    return pl.pallas_call(
        _ada_kernel,
        grid=(n_layers, n // tn),
        in_specs=[
            pl.BlockSpec((SUBLANES, d), lambda l, j: (0, 0)),
            pl.BlockSpec((None, d, tn), lambda l, j: (l, 0, j)),
            pl.BlockSpec((None, 1, tn), lambda l, j: (l, 0, j)),
        ],
        out_specs=pl.BlockSpec((None, SUBLANES, tn), lambda l, j: (l, 0, j)),
        out_shape=jax.ShapeDtypeStruct((n_layers, SUBLANES, n), F32),
        compiler_params=pltpu.CompilerParams(
            dimension_semantics=("arbitrary", "arbitrary"), vmem_limit_bytes=VMEM_LIMIT_BYTES),
        name="ada",
    )(c_pad, w, b)


def _ffn_kernel(x_ref, mod_ref, g_ref, w_in_ref, w_out_ref, o_ref, *, mod_row):
    x = x_ref[...]
    shift = mod_ref[mod_row:mod_row + 1, :]
    scale = mod_ref[mod_row + 1:mod_row + 2, :]
    gate = mod_ref[mod_row + 2:mod_row + 3, :]
    h = _rms_mod(x, g_ref[...], shift, scale).astype(BF16)
    gu = _dot(h, w_in_ref[...])
    act = (_silu(gu[:, :D_FF]) * gu[:, D_FF:]).astype(BF16)
    y = _dot(act, w_out_ref[...])
    o_ref[...] = x + (0.5 * gate) * y


def _ffn_call(x2d, mod, g, w_in, w_out, mod_row, tiles_per_batch):
    n_tok, d = x2d.shape
    resident = pl.Buffered(1)
    return pl.pallas_call(
        functools.partial(_ffn_kernel, mod_row=mod_row),
        grid=(n_tok // FFN_TILE,),
        in_specs=[
            pl.BlockSpec((FFN_TILE, d), lambda i: (i, 0)),
            pl.BlockSpec((None, N_MOD, d), lambda i: (i // tiles_per_batch, 0, 0)),
            pl.BlockSpec((1, d), lambda i: (0, 0)),
            pl.BlockSpec(w_in.shape, lambda i: (0, 0), pipeline_mode=resident),
            pl.BlockSpec(w_out.shape, lambda i: (0, 0), pipeline_mode=resident),
        ],
        out_specs=pl.BlockSpec((FFN_TILE, d), lambda i: (i, 0)),
        out_shape=jax.ShapeDtypeStruct((n_tok, d), F32),
        compiler_params=pltpu.CompilerParams(
            dimension_semantics=("arbitrary",), vmem_limit_bytes=VMEM_LIMIT_BYTES),
        name="ffn",
    )(x2d, mod, g, w_in, w_out)


def _conv_group(z, ubuf_ref, wdw_ref, bdw_ref, gln_ref, bln_ref):
    tile = z.shape[0]
    ubuf_ref[CONV_PAD:CONV_PAD + tile, :] = z[:, :D_CONV] * _sigmoid(z[:, D_CONV:2 * D_CONV])
    first = CONV_PAD - (CONV_WIDTH - 1)
    blocks = []
    for r0 in range(0, tile, CONV_ROWS):
        acc = jnp.zeros((CONV_ROWS, D_CONV), F32)
        for k in range(CONV_WIDTH):
            acc = acc + wdw_ref[k:k + 1, :] * ubuf_ref[first + r0 + k:first + r0 + k + CONV_ROWS, :]
        blocks.append(acc)
    y = jnp.concatenate(blocks, axis=0) + bdw_ref[...]
    ubuf_ref[0:CONV_PAD, :] = ubuf_ref[tile:tile + CONV_PAD, :]
    yc = y - jnp.mean(y, axis=-1, keepdims=True)
    yn = yc * lax.rsqrt(jnp.mean(yc * yc, axis=-1, keepdims=True) + EPS)
    return _silu(yn * gln_ref[...] + bln_ref[...])


def _gla_subtile(q, k, v, log_a, s_ref, gnorm_ref):
    t = GLA_TILE
    row = lax.broadcasted_iota(jnp.int32, (t, t), 0)
    col = lax.broadcasted_iota(jnp.int32, (t, t), 1)
    lower = col <= row
    tri = jnp.where(lower, 1.0, 0.0).astype(BF16)
    ones = jnp.ones((t, LANES), BF16)
    parts = _split_bf16(log_a, 3)
    cum = sum(_dot(tri, p) for p in parts)
    total_cols = sum(_dot_tn(p, ones) for p in parts)
    mid = cum[CHUNK - 1:CHUNK, :]
    end = cum[t - 1:t, :]
    rel = cum - mid
    e_pos = jnp.exp(rel)
    e_neg = jnp.exp(-rel)
    qs = q * (GLA_DK ** -0.5)
    q_fwd = (qs * e_pos).astype(BF16)
    q_bwd = (qs * e_neg).astype(BF16)
    q_in = (qs * jnp.exp(cum)).astype(BF16)
    k_fwd = (k * e_pos).astype(BF16)
    k_bwd = (k * e_neg).astype(BF16)
    k_out = (k * jnp.exp(end - cum)).astype(BF16)
    v16 = v.astype(BF16)

    same_chunk = (row >= CHUNK) == (col >= CHUNK)
    lane_head = lax.broadcasted_iota(jnp.int32, (t, D_QK), 1) // GLA_DK
    s_old = s_ref[...]
    s16 = s_old.astype(BF16)
    zero16 = jnp.zeros((), BF16)
    outs = []
    for h in range(GLA_HEADS):
        in_head = lane_head == h
        vh = v16[:, h * GLA_DV:(h + 1) * GLA_DV]
        att_fwd = _dot_nt(jnp.where(in_head, q_fwd, zero16), k_bwd)
        att_bwd = _dot_nt(jnp.where(in_head, q_bwd, zero16), k_fwd)
        att = jnp.where(lower, att_fwd, jnp.where(same_chunk, att_bwd, 0.0))
        o = _dot(att.astype(BF16), vh) + _dot(jnp.where(in_head, q_in, zero16), s16)
        o = o * lax.rsqrt(jnp.mean(o * o, axis=-1, keepdims=True) + EPS) * gnorm_ref[h:h + 1, :]
        outs.append(o)
    upd = _dot_tn(k_out, v16)
    decay = jnp.exp(total_cols)
    for h in range(GLA_HEADS):
        rows = slice(h * GLA_DK, (h + 1) * GLA_DK)
        s_ref[rows, :] = decay[rows, :] * s_old[rows, :] + upd[rows, h * GLA_DV:(h + 1) * GLA_DV]
    return jnp.concatenate(outs, axis=-1)


def _mixer_kernel(x_ref, mod_ref, g_ref, w_main_ref, w_glr_ref, wgu_ref, bg_ref, wdw_ref, bdw_ref,
                  gln_ref, bln_ref, gnorm_ref, w_out_ref, o_ref, s_ref, ubuf_ref, mix_ref):
    @pl.when(pl.program_id(1) == 0)
    def _():
        s_ref[...] = jnp.zeros_like(s_ref)
        ubuf_ref[0:CONV_PAD, :] = jnp.zeros((CONV_PAD, D_CONV), F32)

    x = x_ref[...]
    tile = x.shape[0]
    h = _rms_mod(x, g_ref[...], mod_ref[3:4, :], mod_ref[4:5, :]).astype(BF16)
    z = _dot(h, w_main_ref[...])
    glr = _dot(h, w_glr_ref[...])
    mix_ref[:, :D_CONV] = _conv_group(z, ubuf_ref, wdw_ref, bdw_ref, gln_ref, bln_ref).astype(BF16)

    log_a = _log_sigmoid(_dot(glr.astype(BF16), wgu_ref[...]) + bg_ref[...]) * (1.0 / GATE_TAU)
    q0, k0, v0, r0 = 2 * D_CONV, 2 * D_CONV + D_QK, 2 * D_CONV + 2 * D_QK, 2 * D_CONV + 2 * D_QK + D_GLA
    for t0 in range(0, tile, GLA_TILE):
        rows = slice(t0, t0 + GLA_TILE)
        o = _gla_subtile(z[rows, q0:k0], z[rows, k0:v0], z[rows, v0:r0], log_a[rows, :], s_ref, gnorm_ref)
        mix_ref[rows, D_CONV:] = (o * _silu(z[rows, r0:])).astype(BF16)

    y = _dot(mix_ref[...], w_out_ref[...])
    o_ref[...] = x + mod_ref[5:6, :] * y


def _mixer_call(x, mod, g, w_main, w_glr, wgu, bg, wdw, bdw, gln, bln, gnorm, w_out):
    bsz, seq, d = x.shape
    resident = pl.Buffered(1)

    def whole(a):
        return pl.BlockSpec(a.shape, lambda b, t: (0,) * a.ndim, pipeline_mode=resident)

    return pl.pallas_call(
        _mixer_kernel,
        grid=(bsz, seq // MIX_TILE),
        in_specs=[
            pl.BlockSpec((None, MIX_TILE, d), lambda b, t: (b, t, 0)),
            pl.BlockSpec((None, N_MOD, d), lambda b, t: (b, 0, 0)),
            whole(g), whole(w_main), whole(w_glr), whole(wgu), whole(bg), whole(wdw), whole(bdw),
            whole(gln), whole(bln), whole(gnorm), whole(w_out),
        ],
        out_specs=pl.BlockSpec((None, MIX_TILE, d), lambda b, t: (b, t, 0)),
        out_shape=jax.ShapeDtypeStruct((bsz, seq, d), F32),
        scratch_shapes=[
            pltpu.VMEM((D_QK, GLA_DV), F32),
            pltpu.VMEM((CONV_PAD + MIX_TILE, D_CONV), F32),
            pltpu.VMEM((MIX_TILE, D_MIX), BF16),
        ],
        compiler_params=pltpu.CompilerParams(
            dimension_semantics=("arbitrary", "arbitrary"), vmem_limit_bytes=VMEM_LIMIT_BYTES),
        name="mixer",
    )(x, mod, g, w_main, w_glr, wgu, bg, wdw, bdw, gln, bln, gnorm, w_out)


def _final_kernel(x_ref, mod_ref, g_ref, o_ref):
    o_ref[...] = _rms_mod(x_ref[...], g_ref[...], mod_ref[0:1, :], mod_ref[1:2, :])


def _final_call(x2d, fmod, g, tiles_per_batch):
    n_tok, d = x2d.shape
    return pl.pallas_call(
        _final_kernel,
        grid=(n_tok // FFN_TILE,),
        in_specs=[
            pl.BlockSpec((FFN_TILE, d), lambda i: (i, 0)),
            pl.BlockSpec((None, 2, d), lambda i: (i // tiles_per_batch, 0, 0)),
            pl.BlockSpec((1, d), lambda i: (0, 0)),
        ],
        out_specs=pl.BlockSpec((FFN_TILE, d), lambda i: (i, 0)),
        out_shape=jax.ShapeDtypeStruct((n_tok, d), F32),
        compiler_params=pltpu.CompilerParams(
            dimension_semantics=("arbitrary",), vmem_limit_bytes=VMEM_LIMIT_BYTES),
        name="final_norm",
    )(x2d, fmod, g)


def kernel(x, c, w_ada, b_ada, g_norm_ffn1, w_ffn1_in, w_ffn1_out, g_norm_mix, w_in, w_dw, b_dw, g_conv_ln, b_conv_ln, w_gate_up, b_gate, g_gla_norm, w_out, g_norm_ffn2, w_ffn2_in, w_ffn2_out, g_norm_final, w_ada_final, b_ada_final):
    bsz, seq, d = x.shape
    depth = w_ada.shape[0]
    assert seq % MIX_TILE == 0 and seq % FFN_TILE == 0 and MIX_TILE % GLA_TILE == 0 and MIX_TILE % CONV_ROWS == 0
    assert bsz <= SUBLANES
    tiles_per_batch = seq // FFN_TILE

    c_pad = jnp.pad(c, ((0, SUBLANES - bsz), (0, 0)))
    mod = _ada_call(c_pad, w_ada, b_ada[:, None, :], 1024)[:, :bsz].reshape(depth, bsz, N_MOD, d)
    fmod = _ada_call(c_pad, w_ada_final[None], b_ada_final[None, None, :], 1024)[0, :bsz].reshape(bsz, 2, d)

    w_glr = jnp.pad(w_in[:, :, D_MAIN:], ((0, 0), (0, 0), (0, LANES - GATE_RANK))).astype(BF16)
    wgu = jnp.pad(w_gate_up, ((0, 0), (0, LANES - GATE_RANK), (0, 0))).astype(BF16)

    x2d = x.reshape(bsz * seq, d)
    for l in range(depth):
        x2d = _ffn_call(x2d, mod[l], g_norm_ffn1[l][None], w_ffn1_in[l].astype(BF16),
                        w_ffn1_out[l].astype(BF16), 0, tiles_per_batch)
        x3d = _mixer_call(
            x2d.reshape(bsz, seq, d), mod[l], g_norm_mix[l][None], w_in[l, :, :D_MAIN].astype(BF16), w_glr[l],
            wgu[l], b_gate[l][None], w_dw[l], b_dw[l][None], g_conv_ln[l][None], b_conv_ln[l][None],
            g_gla_norm[l], w_out[l].astype(BF16))
        x2d = _ffn_call(x3d.reshape(bsz * seq, d), mod[l], g_norm_ffn2[l][None], w_ffn2_in[l].astype(BF16),
                        w_ffn2_out[l].astype(BF16), 6, tiles_per_batch)
    out = _final_call(x2d, fmod, g_norm_final[None], tiles_per_batch)
    return out.reshape(bsz, seq, d)
```

```python
import functools

import jax
import jax.numpy as jnp
from jax import lax
from jax.experimental import pallas as pl
from jax.experimental.pallas import tpu as pltpu

D_MODEL = 1024
D_FF = 2816
D_CONV = 512
CONV_WIDTH = 31
GLA_HEADS = 4
GLA_DK = 64
GLA_DV = 128
D_QK = GLA_HEADS * GLA_DK
D_GLA = GLA_HEADS * GLA_DV
D_MIX = D_CONV + D_GLA
GATE_RANK = 16
GATE_TAU = 16.0
CHUNK = 64
N_MOD = 9
EPS = 1e-6
D_MAIN = 2 * D_CONV + 2 * D_QK + 2 * D_GLA

LANES = 128
SUBLANES = 8
VMEM_LIMIT_BYTES = 56 * 1024 * 1024

FFN_TILE = 512
MIX_TILE = 256
GLA_TILE = 2 * CHUNK
CONV_ROWS = 32
CONV_PAD = 32

F32 = jnp.float32
BF16 = jnp.bfloat16


def _dot(a, b):
    return jnp.dot(a, b, preferred_element_type=F32)


def _dot_nt(a, b):
    return lax.dot_general(a, b, (((1,), (1,)), ((), ())), preferred_element_type=F32)


def _dot_tn(a, b):
    return lax.dot_general(a, b, (((0,), (0,)), ((), ())), preferred_element_type=F32)


def _sigmoid(x):
    return 1.0 / (1.0 + jnp.exp(-x))


def _silu(x):
    return x * _sigmoid(x)


def _log_sigmoid(x):
    return jnp.minimum(x, 0.0) - jnp.log(1.0 + jnp.exp(-jnp.abs(x)))


def _rms_mod(x, g, shift, scale):
    y = x * lax.rsqrt(jnp.mean(x * x, axis=-1, keepdims=True) + EPS)
    return (y * g) * (1.0 + scale) + shift


def _split_bf16(x, parts):
    out = []
    for _ in range(parts):
        p = x.astype(BF16)
        out.append(p)
        x = x - p.astype(F32)
    return out


def _ada_kernel(c_ref, w_ref, b_ref, o_ref):
    c = _silu(c_ref[...])
    o_ref[...] = _dot(c.astype(BF16), w_ref[...].astype(BF16)) + b_ref[...]


def _ada_call(c_pad, w, b, tn):
    n_layers, d, n = w.shape
    return pl.pallas_call(
        _ada_kernel,
        grid=(n_layers, n // tn),
        in_specs=[
            pl.BlockSpec((SUBLANES, d), lambda l, j: (0, 0)),
            pl.BlockSpec((None, d, tn), lambda l, j: (l, 0, j)),
            pl.BlockSpec((None, 1, tn), lambda l, j: (l, 0, j)),
        ],
        out_specs=pl.BlockSpec((None, SUBLANES, tn), lambda l, j: (l, 0, j)),
        out_shape=jax.ShapeDtypeStruct((n_layers, SUBLANES, n), F32),
        compiler_params=pltpu.CompilerParams(
            dimension_semantics=("arbitrary", "arbitrary"), vmem_limit_bytes=VMEM_LIMIT_BYTES),
        name="ada",
    )(c_pad, w, b)


def _ffn_kernel(x_ref, mod_ref, g_ref, w_in_ref, w_out_ref, o_ref, *, mod_row, final):
    x = x_ref[...]
    shift = mod_ref[mod_row:mod_row + 1, :]
    scale = mod_ref[mod_row + 1:mod_row + 2, :]
    gate = mod_ref[mod_row + 2:mod_row + 3, :]
    h = _rms_mod(x, g_ref[0:1, :], shift, scale).astype(BF16)
    gu = _dot(h, w_in_ref[...])
    act = (_silu(gu[:, :D_FF]) * gu[:, D_FF:]).astype(BF16)
    y = x + (0.5 * gate) * _dot(act, w_out_ref[...])
    if final:
        y = _rms_mod(y, g_ref[1:2, :], mod_ref[N_MOD:N_MOD + 1, :], mod_ref[N_MOD + 1:N_MOD + 2, :])
    o_ref[...] = y


def _ffn_call(x2d, mod, g, w_in, w_out, mod_row, tiles_per_batch, final):
    n_tok, d = x2d.shape
    resident = pl.Buffered(1)
    return pl.pallas_call(
        functools.partial(_ffn_kernel, mod_row=mod_row, final=final),
        grid=(n_tok // FFN_TILE,),
        in_specs=[
            pl.BlockSpec((FFN_TILE, d), lambda i: (i, 0)),
            pl.BlockSpec((None, N_MOD + 2, d), lambda i: (i // tiles_per_batch, 0, 0)),
            pl.BlockSpec((2, d), lambda i: (0, 0)),
            pl.BlockSpec(w_in.shape, lambda i: (0, 0), pipeline_mode=resident),
            pl.BlockSpec(w_out.shape, lambda i: (0, 0), pipeline_mode=resident),
        ],
        out_specs=pl.BlockSpec((FFN_TILE, d), lambda i: (i, 0)),
        out_shape=jax.ShapeDtypeStruct((n_tok, d), F32),
        compiler_params=pltpu.CompilerParams(
            dimension_semantics=("arbitrary",), vmem_limit_bytes=VMEM_LIMIT_BYTES),
        name="ffn",
    )(x2d, mod, g, w_in, w_out)


def _conv_group(z, ubuf_ref, ushift_ref, wdw_ref, bdw_ref, gln_ref, bln_ref):
    tile = z.shape[0]
    ubuf_ref[CONV_PAD:CONV_PAD + tile, :] = z[:, :D_CONV] * _sigmoid(z[:, D_CONV:2 * D_CONV])
    first = CONV_PAD - (CONV_WIDTH - 1)
    shifted_rows = ushift_ref.shape[1]
    for s in range(1, SUBLANES):
        ushift_ref[s - 1] = ubuf_ref[s:s + shifted_rows, :]
    blocks = []
    for r0 in range(0, tile, CONV_ROWS):
        acc = jnp.zeros((CONV_ROWS, D_CONV), F32)
        for k in range(CONV_WIDTH):
            base, s = (first + k) // SUBLANES * SUBLANES, (first + k) % SUBLANES
            rows = slice(r0 + base, r0 + base + CONV_ROWS)
            tap = ubuf_ref[rows, :] if s == 0 else ushift_ref[s - 1, rows, :]
            acc = acc + wdw_ref[k:k + 1, :] * tap
        blocks.append(acc)
    y = jnp.concatenate(blocks, axis=0) + bdw_ref[...]
    ubuf_ref[0:CONV_PAD, :] = ubuf_ref[tile:tile + CONV_PAD, :]
    yc = y - jnp.mean(y, axis=-1, keepdims=True)
    yn = yc * lax.rsqrt(jnp.mean(yc * yc, axis=-1, keepdims=True) + EPS)
    return _silu(yn * gln_ref[...] + bln_ref[...])


def _gla_subtile(q, k, v, log_a, s_ref, gnorm_ref):
    t = GLA_TILE
    row = lax.broadcasted_iota(jnp.int32, (t, t), 0)
    col = lax.broadcasted_iota(jnp.int32, (t, t), 1)
    lower = col <= row
    tri = jnp.where(lower, 1.0, 0.0).astype(BF16)
    ones = jnp.ones((t, LANES), BF16)
    parts = _split_bf16(log_a, 3)
    cum = sum(_dot(tri, p) for p in parts)
    total_cols = sum(_dot_tn(p, ones) for p in parts)
    mid = cum[CHUNK - 1:CHUNK, :]
    end = cum[t - 1:t, :]
    rel = cum - mid
    e_pos = jnp.exp(rel)
    e_neg = jnp.exp(-rel)
    qs = q * (GLA_DK ** -0.5)
    q_fwd = (qs * e_pos).astype(BF16)
    q_bwd = (qs * e_neg).astype(BF16)
    q_in = (qs * jnp.exp(cum)).astype(BF16)
    k_fwd = (k * e_pos).astype(BF16)
    k_bwd = (k * e_neg).astype(BF16)
    k_out = (k * jnp.exp(end - cum)).astype(BF16)
    v16 = v.astype(BF16)

    same_chunk = (row >= CHUNK) == (col >= CHUNK)
    lane_head = lax.broadcasted_iota(jnp.int32, (t, D_QK), 1) // GLA_DK
    s_old = s_ref[...]
    s16 = s_old.astype(BF16)
    zero16 = jnp.zeros((), BF16)
    outs = []
    for h in range(GLA_HEADS):
        in_head = lane_head == h
        vh = v16[:, h * GLA_DV:(h + 1) * GLA_DV]
        att_fwd = _dot_nt(jnp.where(in_head, q_fwd, zero16), k_bwd)
        att_bwd = _dot_nt(jnp.where(in_head, q_bwd, zero16), k_fwd)
        att = jnp.where(lower, att_fwd, jnp.where(same_chunk, att_bwd, 0.0))
        o = _dot(att.astype(BF16), vh) + _dot(jnp.where(in_head, q_in, zero16), s16)
        o = o * lax.rsqrt(jnp.mean(o * o, axis=-1, keepdims=True) + EPS) * gnorm_ref[h:h + 1, :]
        outs.append(o)
    upd = _dot_tn(k_out, v16)
    decay = jnp.exp(total_cols)
    for h in range(GLA_HEADS):
        rows = slice(h * GLA_DK, (h + 1) * GLA_DK)
        s_ref[rows, :] = decay[rows, :] * s_old[rows, :] + upd[rows, h * GLA_DV:(h + 1) * GLA_DV]
    return jnp.concatenate(outs, axis=-1)


def _mixer_kernel(x_ref, mod_ref, g_ref, w_main_ref, w_glr_ref, wgu_ref, bg_ref, wdw_ref, bdw_ref,
                  gln_ref, bln_ref, gnorm_ref, w_out_ref, o_ref, s_ref, ubuf_ref, ushift_ref, mix_ref):
    @pl.when(pl.program_id(1) == 0)
    def _():
        s_ref[...] = jnp.zeros_like(s_ref)
        ubuf_ref[0:CONV_PAD, :] = jnp.zeros((CONV_PAD, D_CONV), F32)

    x = x_ref[...]
    tile = x.shape[0]
    h = _rms_mod(x, g_ref[...], mod_ref[3:4, :], mod_ref[4:5, :]).astype(BF16)
    z = _dot(h, w_main_ref[...])
    glr = _dot(h, w_glr_ref[...])
    mix_ref[:, :D_CONV] = _conv_group(z, ubuf_ref, ushift_ref, wdw_ref, bdw_ref, gln_ref, bln_ref).astype(BF16)

    log_a = _log_sigmoid(_dot(glr.astype(BF16), wgu_ref[...]) + bg_ref[...]) * (1.0 / GATE_TAU)
    q0, k0, v0, r0 = 2 * D_CONV, 2 * D_CONV + D_QK, 2 * D_CONV + 2 * D_QK, 2 * D_CONV + 2 * D_QK + D_GLA
    for t0 in range(0, tile, GLA_TILE):
        rows = slice(t0, t0 + GLA_TILE)
        o = _gla_subtile(z[rows, q0:k0], z[rows, k0:v0], z[rows, v0:r0], log_a[rows, :], s_ref, gnorm_ref)
        mix_ref[rows, D_CONV:] = (o * _silu(z[rows, r0:])).astype(BF16)

    y = _dot(mix_ref[...], w_out_ref[...])
    o_ref[...] = x + mod_ref[5:6, :] * y


def _mixer_call(x, mod, g, w_main, w_glr, wgu, bg, wdw, bdw, gln, bln, gnorm, w_out):
    bsz, seq, d = x.shape
    resident = pl.Buffered(1)

    def whole(a):
        return pl.BlockSpec(a.shape, lambda b, t: (0,) * a.ndim, pipeline_mode=resident)

    return pl.pallas_call(
        _mixer_kernel,
        grid=(bsz, seq // MIX_TILE),
        in_specs=[
            pl.BlockSpec((None, MIX_TILE, d), lambda b, t: (b, t, 0)),
            pl.BlockSpec((None, N_MOD, d), lambda b, t: (b, 0, 0)),
            whole(g), whole(w_main), whole(w_glr), whole(wgu), whole(bg), whole(wdw), whole(bdw),
            whole(gln), whole(bln), whole(gnorm), whole(w_out),
        ],
        out_specs=pl.BlockSpec((None, MIX_TILE, d), lambda b, t: (b, t, 0)),
        out_shape=jax.ShapeDtypeStruct((bsz, seq, d), F32),
        scratch_shapes=[
            pltpu.VMEM((D_QK, GLA_DV), F32),
            pltpu.VMEM((CONV_PAD + MIX_TILE, D_CONV), F32),
            pltpu.VMEM((SUBLANES - 1, MIX_TILE + CONV_PAD - SUBLANES, D_CONV), F32),
            pltpu.VMEM((MIX_TILE, D_MIX), BF16),
        ],
        compiler_params=pltpu.CompilerParams(
            dimension_semantics=("arbitrary", "arbitrary"), vmem_limit_bytes=VMEM_LIMIT_BYTES),
        name="mixer",
    )(x, mod, g, w_main, w_glr, wgu, bg, wdw, bdw, gln, bln, gnorm, w_out)


def kernel(x, c, w_ada, b_ada, g_norm_ffn1, w_ffn1_in, w_ffn1_out, g_norm_mix, w_in, w_dw, b_dw, g_conv_ln, b_conv_ln, w_gate_up, b_gate, g_gla_norm, w_out, g_norm_ffn2, w_ffn2_in, w_ffn2_out, g_norm_final, w_ada_final, b_ada_final):
    bsz, seq, d = x.shape
    depth = w_ada.shape[0]
    assert seq % MIX_TILE == 0 and seq % FFN_TILE == 0 and MIX_TILE % GLA_TILE == 0 and MIX_TILE % CONV_ROWS == 0
    assert bsz <= SUBLANES
    tiles_per_batch = seq // FFN_TILE

    c_pad = jnp.pad(c, ((0, SUBLANES - bsz), (0, 0)))
    mod = _ada_call(c_pad, w_ada, b_ada[:, None, :], 1024)[:, :bsz].reshape(depth, bsz, N_MOD, d)
    fmod = _ada_call(c_pad, w_ada_final[None], b_ada_final[None, None, :], 1024)[0, :bsz].reshape(bsz, 2, d)

    w_glr = jnp.pad(w_in[:, :, D_MAIN:], ((0, 0), (0, 0), (0, LANES - GATE_RANK))).astype(BF16)
    wgu = jnp.pad(w_gate_up, ((0, 0), (0, LANES - GATE_RANK), (0, 0))).astype(BF16)

    x2d = x.reshape(bsz * seq, d)
    for l in range(depth):
        mod_l = jnp.concatenate([mod[l], fmod], axis=1)
        x2d = _ffn_call(x2d, mod_l, jnp.stack([g_norm_ffn1[l], g_norm_final]), w_ffn1_in[l].astype(BF16),
                        w_ffn1_out[l].astype(BF16), 0, tiles_per_batch, False)
        x3d = _mixer_call(
            x2d.reshape(bsz, seq, d), mod[l], g_norm_mix[l][None], w_in[l, :, :D_MAIN].astype(BF16), w_glr[l],
            wgu[l], b_gate[l][None], w_dw[l], b_dw[l][None], g_conv_ln[l][None], b_conv_ln[l][None],
            g_gla_norm[l], w_out[l].astype(BF16))
        x2d = _ffn_call(x3d.reshape(bsz * seq, d), mod_l, jnp.stack([g_norm_ffn2[l], g_norm_final]),
                        w_ffn2_in[l].astype(BF16), w_ffn2_out[l].astype(BF16), 6, tiles_per_batch, l == depth - 1)
    return x2d.reshape(bsz, seq, d)
```

```python
import functools

import jax
import jax.numpy as jnp
from jax import lax
from jax.experimental import pallas as pl
from jax.experimental.pallas import tpu as pltpu

D_MODEL = 1024
D_FF = 2816
D_CONV = 512
CONV_WIDTH = 31
GLA_HEADS = 4
GLA_DK = 64
GLA_DV = 128
D_QK = GLA_HEADS * GLA_DK
D_GLA = GLA_HEADS * GLA_DV
D_MIX = D_CONV + D_GLA
GATE_RANK = 16
GATE_TAU = 16.0
CHUNK = 64
N_MOD = 9
EPS = 1e-6
D_MAIN = 2 * D_CONV + 2 * D_QK + 2 * D_GLA

LANES = 128
SUBLANES = 8
VMEM_LIMIT_BYTES = 56 * 1024 * 1024

FFN_TILE = 512
MIX_TILE = 256
GLA_TILE = 2 * CHUNK
CONV_ROWS = 32
CONV_PAD = 32
W_STAGE_CHUNKS = 8

F32 = jnp.float32
BF16 = jnp.bfloat16


def _dot(a, b):
    return jnp.dot(a, b, preferred_element_type=F32)


def _dot_nt(a, b):
    return lax.dot_general(a, b, (((1,), (1,)), ((), ())), preferred_element_type=F32)


def _dot_tn(a, b):
    return lax.dot_general(a, b, (((0,), (0,)), ((), ())), preferred_element_type=F32)


def _sigmoid(x):
    return 1.0 / (1.0 + jnp.exp(-x))


def _silu(x):
    return x * _sigmoid(x)


def _log_sigmoid(x):
    return jnp.minimum(x, 0.0) - jnp.log(1.0 + jnp.exp(-jnp.abs(x)))


def _rms_mod(x, g, shift, scale):
    y = x * lax.rsqrt(jnp.mean(x * x, axis=-1, keepdims=True) + EPS)
    return (y * g) * (1.0 + scale) + shift


def _split_bf16(x, parts):
    out = []
    for _ in range(parts):
        p = x.astype(BF16)
        out.append(p)
        x = x - p.astype(F32)
    return out


def _ada_kernel(c_ref, w_ref, b_ref, o_ref):
    c = _silu(c_ref[...])
    o_ref[...] = _dot(c.astype(BF16), w_ref[...].astype(BF16)) + b_ref[...]


def _ada_call(c_pad, w, b, tn):
    n_layers, d, n = w.shape
    return pl.pallas_call(
        _ada_kernel,
        grid=(n_layers, n // tn),
        in_specs=[
            pl.BlockSpec((SUBLANES, d), lambda l, j: (0, 0)),
            pl.BlockSpec((None, d, tn), lambda l, j: (l, 0, j)),
            pl.BlockSpec((None, 1, tn), lambda l, j: (l, 0, j)),
        ],
        out_specs=pl.BlockSpec((None, SUBLANES, tn), lambda l, j: (l, 0, j)),
        out_shape=jax.ShapeDtypeStruct((n_layers, SUBLANES, n), F32),
        compiler_params=pltpu.CompilerParams(
            dimension_semantics=("arbitrary", "arbitrary"), vmem_limit_bytes=VMEM_LIMIT_BYTES),
        name="ada",
    )(c_pad, w, b)


def _load_as_bf16(src_hbm_ref, dst_ref, stage_ref, sem_ref):
    rows = stage_ref.shape[1]
    n_chunks = src_hbm_ref.shape[0] // rows
    assert n_chunks * rows == src_hbm_ref.shape[0] == dst_ref.shape[0]

    def chunk_copy(c):
        return pltpu.make_async_copy(src_hbm_ref.at[pl.ds(c * rows, rows), :], stage_ref.at[c % 2], sem_ref.at[c % 2])

    chunk_copy(0).start()
    for c in range(n_chunks):
        if c + 1 < n_chunks:
            chunk_copy(c + 1).start()
        chunk_copy(c).wait()
        dst_ref[c * rows:(c + 1) * rows, :] = stage_ref[c % 2].astype(BF16)


def _ffn_kernel(x_ref, mod_ref, g_ref, w_in_hbm_ref, w_out_hbm_ref, o_ref,
                w_in_ref, w_out_ref, stage_in_ref, stage_out_ref, sem_in_ref, sem_out_ref, *, layer, mod_row, final):
    @pl.when(pl.program_id(0) == 0)
    def _():
        _load_as_bf16(w_in_hbm_ref.at[layer], w_in_ref, stage_in_ref, sem_in_ref)
        _load_as_bf16(w_out_hbm_ref.at[layer], w_out_ref, stage_out_ref, sem_out_ref)

    x = x_ref[...]
    shift = mod_ref[mod_row:mod_row + 1, :]
    scale = mod_ref[mod_row + 1:mod_row + 2, :]
    gate = mod_ref[mod_row + 2:mod_row + 3, :]
    h = _rms_mod(x, g_ref[0:1, :], shift, scale).astype(BF16)
    gu = _dot(h, w_in_ref[...])
    act = (_silu(gu[:, :D_FF]) * gu[:, D_FF:]).astype(BF16)
    y = x + (0.5 * gate) * _dot(act, w_out_ref[...])
    if final:
        y = _rms_mod(y, g_ref[1:2, :], mod_ref[N_MOD:N_MOD + 1, :], mod_ref[N_MOD + 1:N_MOD + 2, :])
    o_ref[...] = y


def _ffn_call(x2d, mod, g, w_in_all, w_out_all, layer, mod_row, tiles_per_batch, final):
    n_tok, d = x2d.shape
    w_in_shape, w_out_shape = w_in_all.shape[1:], w_out_all.shape[1:]
    assert w_in_shape[0] % W_STAGE_CHUNKS == 0 and w_out_shape[0] % W_STAGE_CHUNKS == 0
    return pl.pallas_call(
        functools.partial(_ffn_kernel, layer=layer, mod_row=mod_row, final=final),
        grid=(n_tok // FFN_TILE,),
        in_specs=[
            pl.BlockSpec((FFN_TILE, d), lambda i: (i, 0)),
            pl.BlockSpec((None, N_MOD + 2, d), lambda i: (i // tiles_per_batch, 0, 0)),
            pl.BlockSpec((2, d), lambda i: (0, 0)),
            pl.BlockSpec(memory_space=pl.ANY),
            pl.BlockSpec(memory_space=pl.ANY),
        ],
        out_specs=pl.BlockSpec((FFN_TILE, d), lambda i: (i, 0)),
        out_shape=jax.ShapeDtypeStruct((n_tok, d), F32),
        scratch_shapes=[
            pltpu.VMEM(w_in_shape, BF16),
            pltpu.VMEM(w_out_shape, BF16),
            pltpu.VMEM((2, w_in_shape[0] // W_STAGE_CHUNKS, w_in_shape[1]), F32),
            pltpu.VMEM((2, w_out_shape[0] // W_STAGE_CHUNKS, w_out_shape[1]), F32),
            pltpu.SemaphoreType.DMA((2,)),
            pltpu.SemaphoreType.DMA((2,)),
        ],
        compiler_params=pltpu.CompilerParams(
            dimension_semantics=("arbitrary",), vmem_limit_bytes=VMEM_LIMIT_BYTES),
        name="ffn",
    )(x2d, mod, g, w_in_all, w_out_all)


def _conv_group(z, ubuf_ref, ushift_ref, wdw_ref, bdw_ref, gln_ref, bln_ref):
    tile = z.shape[0]
    ubuf_ref[CONV_PAD:CONV_PAD + tile, :] = z[:, :D_CONV] * _sigmoid(z[:, D_CONV:2 * D_CONV])
    first = CONV_PAD - (CONV_WIDTH - 1)
    shifted_rows = ushift_ref.shape[1]
    for s in range(1, SUBLANES):
        ushift_ref[s - 1] = ubuf_ref[s:s + shifted_rows, :]
    blocks = []
    for r0 in range(0, tile, CONV_ROWS):
        acc = jnp.zeros((CONV_ROWS, D_CONV), F32)
        for k in range(CONV_WIDTH):
            base, s = (first + k) // SUBLANES * SUBLANES, (first + k) % SUBLANES
            rows = slice(r0 + base, r0 + base + CONV_ROWS)
            tap = ubuf_ref[rows, :] if s == 0 else ushift_ref[s - 1, rows, :]
            acc = acc + wdw_ref[k:k + 1, :] * tap
        blocks.append(acc)
    y = jnp.concatenate(blocks, axis=0) + bdw_ref[...]
    ubuf_ref[0:CONV_PAD, :] = ubuf_ref[tile:tile + CONV_PAD, :]
    yc = y - jnp.mean(y, axis=-1, keepdims=True)
    yn = yc * lax.rsqrt(jnp.mean(yc * yc, axis=-1, keepdims=True) + EPS)
    return _silu(yn * gln_ref[...] + bln_ref[...])


def _gla_subtile(q, k, v, log_a, s_ref, gnorm_ref):
    t = GLA_TILE
    row = lax.broadcasted_iota(jnp.int32, (t, t), 0)
    col = lax.broadcasted_iota(jnp.int32, (t, t), 1)
    lower = col <= row
    tri = jnp.where(lower, 1.0, 0.0).astype(BF16)
    ones = jnp.ones((t, LANES), BF16)
    parts = _split_bf16(log_a, 2)
    cum = sum(_dot(tri, p) for p in parts)
    total_cols = sum(_dot_tn(p, ones) for p in parts)
    mid = cum[CHUNK - 1:CHUNK, :]
    end = cum[t - 1:t, :]
    rel = cum - mid
    e_pos = jnp.exp(rel)
    e_neg = jnp.exp(-rel)
    qs = q * (GLA_DK ** -0.5)
    q_fwd = (qs * e_pos).astype(BF16)
    q_bwd = (qs * e_neg).astype(BF16)
    q_in = (qs * jnp.exp(cum)).astype(BF16)
    k_fwd = (k * e_pos).astype(BF16)
    k_bwd = (k * e_neg).astype(BF16)
    k_out = (k * jnp.exp(end - cum)).astype(BF16)
    v16 = v.astype(BF16)

    same_chunk = (row >= CHUNK) == (col >= CHUNK)
    lane_head = lax.broadcasted_iota(jnp.int32, (t, D_QK), 1) // GLA_DK
    s_old = s_ref[...]
    s16 = s_old.astype(BF16)
    zero16 = jnp.zeros((), BF16)
    outs = []
    for h in range(GLA_HEADS):
        in_head = lane_head == h
        vh = v16[:, h * GLA_DV:(h + 1) * GLA_DV]
        att_fwd = _dot_nt(jnp.where(in_head, q_fwd, zero16), k_bwd)
        att_bwd = _dot_nt(jnp.where(in_head, q_bwd, zero16), k_fwd)
        att = jnp.where(lower, att_fwd, jnp.where(same_chunk, att_bwd, 0.0))
        o = _dot(att.astype(BF16), vh) + _dot(jnp.where(in_head, q_in, zero16), s16)
        o = o * lax.rsqrt(jnp.mean(o * o, axis=-1, keepdims=True) + EPS) * gnorm_ref[h:h + 1, :]
        outs.append(o)
    upd = _dot_tn(k_out, v16)
    decay = jnp.exp(total_cols)
    for h in range(GLA_HEADS):
        rows = slice(h * GLA_DK, (h + 1) * GLA_DK)
        s_ref[rows, :] = decay[rows, :] * s_old[rows, :] + upd[rows, h * GLA_DV:(h + 1) * GLA_DV]
    return jnp.concatenate(outs, axis=-1)


def _mixer_kernel(x_ref, mod_ref, g_ref, w_main_ref, w_glr_ref, wgu_ref, bg_ref, wdw_ref, bdw_ref,
                  gln_ref, bln_ref, gnorm_ref, w_out_ref, o_ref, s_ref, ubuf_ref, ushift_ref, mix_ref):
    @pl.when(pl.program_id(1) == 0)
    def _():
        s_ref[...] = jnp.zeros_like(s_ref)
        ubuf_ref[0:CONV_PAD, :] = jnp.zeros((CONV_PAD, D_CONV), F32)

    x = x_ref[...]
    tile = x.shape[0]
    h = _rms_mod(x, g_ref[...], mod_ref[3:4, :], mod_ref[4:5, :]).astype(BF16)
    z = _dot(h, w_main_ref[...])
    glr = _dot(h, w_glr_ref[...])
    y_conv = _dot(_conv_group(z, ubuf_ref, ushift_ref, wdw_ref, bdw_ref, gln_ref, bln_ref).astype(BF16),
                  w_out_ref[:D_CONV, :])

    log_a = _log_sigmoid(_dot(glr.astype(BF16), wgu_ref[...]) + bg_ref[...]) * (1.0 / GATE_TAU)
    q0, k0, v0, r0 = 2 * D_CONV, 2 * D_CONV + D_QK, 2 * D_CONV + 2 * D_QK, 2 * D_CONV + 2 * D_QK + D_GLA
    for t0 in range(0, tile, GLA_TILE):
        rows = slice(t0, t0 + GLA_TILE)
        o = _gla_subtile(z[rows, q0:k0], z[rows, k0:v0], z[rows, v0:r0], log_a[rows, :], s_ref, gnorm_ref)
        mix_ref[rows, :] = (o * _silu(z[rows, r0:])).astype(BF16)

    y = y_conv + _dot(mix_ref[...], w_out_ref[D_CONV:, :])
    o_ref[...] = x + mod_ref[5:6, :] * y


def _mixer_call(x, mod, g, w_main, w_glr, wgu, bg, wdw, bdw, gln, bln, gnorm, w_out):
    bsz, seq, d = x.shape
    resident = pl.Buffered(1)

    def whole(a):
        return pl.BlockSpec(a.shape, lambda b, t: (0,) * a.ndim, pipeline_mode=resident)

    return pl.pallas_call(
        _mixer_kernel,
        grid=(bsz, seq // MIX_TILE),
        in_specs=[
            pl.BlockSpec((None, MIX_TILE, d), lambda b, t: (b, t, 0)),
            pl.BlockSpec((None, N_MOD, d), lambda b, t: (b, 0, 0)),
            whole(g), whole(w_main), whole(w_glr), whole(wgu), whole(bg), whole(wdw), whole(bdw),
            whole(gln), whole(bln), whole(gnorm), whole(w_out),
        ],
        out_specs=pl.BlockSpec((None, MIX_TILE, d), lambda b, t: (b, t, 0)),
        out_shape=jax.ShapeDtypeStruct((bsz, seq, d), F32),
        scratch_shapes=[
            pltpu.VMEM((D_QK, GLA_DV), F32),
            pltpu.VMEM((CONV_PAD + MIX_TILE, D_CONV), F32),
            pltpu.VMEM((SUBLANES - 1, MIX_TILE + CONV_PAD - SUBLANES, D_CONV), F32),
            pltpu.VMEM((MIX_TILE, D_GLA), BF16),
        ],
        compiler_params=pltpu.CompilerParams(
            dimension_semantics=("arbitrary", "arbitrary"), vmem_limit_bytes=VMEM_LIMIT_BYTES),
        name="mixer",
    )(x, mod, g, w_main, w_glr, wgu, bg, wdw, bdw, gln, bln, gnorm, w_out)


def kernel(x, c, w_ada, b_ada, g_norm_ffn1, w_ffn1_in, w_ffn1_out, g_norm_mix, w_in, w_dw, b_dw, g_conv_ln, b_conv_ln, w_gate_up, b_gate, g_gla_norm, w_out, g_norm_ffn2, w_ffn2_in, w_ffn2_out, g_norm_final, w_ada_final, b_ada_final):
    bsz, seq, d = x.shape
    depth = w_ada.shape[0]
    assert seq % MIX_TILE == 0 and seq % FFN_TILE == 0 and MIX_TILE % GLA_TILE == 0 and MIX_TILE % CONV_ROWS == 0
    assert bsz <= SUBLANES
    tiles_per_batch = seq // FFN_TILE

    c_pad = jnp.pad(c, ((0, SUBLANES - bsz), (0, 0)))
    mod = _ada_call(c_pad, w_ada, b_ada[:, None, :], 1024)[:, :bsz].reshape(depth, bsz, N_MOD, d)
    fmod = _ada_call(c_pad, w_ada_final[None], b_ada_final[None, None, :], 1024)[0, :bsz].reshape(bsz, 2, d)

    w_glr = jnp.pad(w_in[:, :, D_MAIN:], ((0, 0), (0, 0), (0, LANES - GATE_RANK))).astype(BF16)
    wgu = jnp.pad(w_gate_up, ((0, 0), (0, LANES - GATE_RANK), (0, 0))).astype(BF16)

    x2d = x.reshape(bsz * seq, d)
    for l in range(depth):
        mod_l = jnp.concatenate([mod[l], fmod], axis=1)
        x2d = _ffn_call(x2d, mod_l, jnp.stack([g_norm_ffn1[l], g_norm_final]), w_ffn1_in,
                        w_ffn1_out, l, 0, tiles_per_batch, False)
        x3d = _mixer_call(
            x2d.reshape(bsz, seq, d), mod[l], g_norm_mix[l][None], w_in[l, :, :D_MAIN].astype(BF16), w_glr[l],
            wgu[l], b_gate[l][None], w_dw[l], b_dw[l][None], g_conv_ln[l][None], b_conv_ln[l][None],
            g_gla_norm[l], w_out[l].astype(BF16))
        x2d = _ffn_call(x3d.reshape(bsz * seq, d), mod_l, jnp.stack([g_norm_ffn2[l], g_norm_final]),
                        w_ffn2_in, w_ffn2_out, l, 6, tiles_per_batch, l == depth - 1)
    return x2d.reshape(bsz, seq, d)
```

```python
import functools

import jax
import jax.numpy as jnp
from jax import lax
from jax.experimental import pallas as pl
from jax.experimental.pallas import tpu as pltpu

D_MODEL = 1024
D_FF = 2816
D_CONV = 512
CONV_WIDTH = 31
GLA_HEADS = 4
GLA_DK = 64
GLA_DV = 128
D_QK = GLA_HEADS * GLA_DK
D_GLA = GLA_HEADS * GLA_DV
D_MIX = D_CONV + D_GLA
GATE_RANK = 16
GATE_TAU = 16.0
CHUNK = 64
N_MOD = 9
EPS = 1e-6
D_MAIN = 2 * D_CONV + 2 * D_QK + 2 * D_GLA

LANES = 128
SUBLANES = 8
VMEM_LIMIT_BYTES = 56 * 1024 * 1024

FFN_TILE = 1024
FFN_SUB = 128
MIX_TILE = 512
MIX_SUB = 256
GLA_TILE = 2 * CHUNK
CONV_ROWS = 32
CONV_PAD = 32
W_STAGE_CHUNKS = 8

F32 = jnp.float32
BF16 = jnp.bfloat16


def _dot(a, b):
    return jnp.dot(a, b, preferred_element_type=F32)


def _dot_nt(a, b):
    return lax.dot_general(a, b, (((1,), (1,)), ((), ())), preferred_element_type=F32)


def _dot_tn(a, b):
    return lax.dot_general(a, b, (((0,), (0,)), ((), ())), preferred_element_type=F32)


def _sigmoid(x):
    return 1.0 / (1.0 + jnp.exp(-x))


def _silu(x):
    return x * _sigmoid(x)


def _log_sigmoid(x):
    return jnp.minimum(x, 0.0) - jnp.log(1.0 + jnp.exp(-jnp.abs(x)))


def _rms_mod(x, g, shift, scale):
    y = x * lax.rsqrt(jnp.mean(x * x, axis=-1, keepdims=True) + EPS)
    return (y * g) * (1.0 + scale) + shift


def _split_bf16(x, parts):
    out = []
    for _ in range(parts):
        p = x.astype(BF16)
        out.append(p)
        x = x - p.astype(F32)
    return out


def _ada_kernel(c_ref, w_ref, b_ref, o_ref):
    c = _silu(c_ref[...])
    o_ref[...] = _dot(c.astype(BF16), w_ref[...].astype(BF16)) + b_ref[...]


def _ada_call(c_pad, w, b, tn):
    n_layers, d, n = w.shape
    return pl.pallas_call(
        _ada_kernel,
        grid=(n_layers, n // tn),
        in_specs=[
            pl.BlockSpec((SUBLANES, d), lambda l, j: (0, 0)),
            pl.BlockSpec((None, d, tn), lambda l, j: (l, 0, j)),
            pl.BlockSpec((None, 1, tn), lambda l, j: (l, 0, j)),
        ],
        out_specs=pl.BlockSpec((None, SUBLANES, tn), lambda l, j: (l, 0, j)),
        out_shape=jax.ShapeDtypeStruct((n_layers, SUBLANES, n), F32),
        compiler_params=pltpu.CompilerParams(
            dimension_semantics=("arbitrary", "arbitrary"), vmem_limit_bytes=VMEM_LIMIT_BYTES),
        name="ada",
    )(c_pad, w, b)


def _load_as_bf16(src_hbm_ref, dst_ref, stage_ref, sem_ref):
    rows = stage_ref.shape[1]
    n_chunks = src_hbm_ref.shape[0] // rows
    assert n_chunks * rows == src_hbm_ref.shape[0] == dst_ref.shape[0]

    def chunk_copy(c):
        return pltpu.make_async_copy(src_hbm_ref.at[pl.ds(c * rows, rows), :], stage_ref.at[c % 2], sem_ref.at[c % 2])

    chunk_copy(0).start()
    for c in range(n_chunks):
        if c + 1 < n_chunks:
            chunk_copy(c + 1).start()
        chunk_copy(c).wait()
        dst_ref[c * rows:(c + 1) * rows, :] = stage_ref[c % 2].astype(BF16)


def _ffn_kernel(x_ref, mod_ref, g_ref, w_in_hbm_ref, w_out_hbm_ref, o_ref,
                w_in_ref, w_out_ref, stage_in_ref, stage_out_ref, sem_in_ref, sem_out_ref, *, layer, mod_row, final):
    @pl.when(pl.program_id(0) == 0)
    def _():
        _load_as_bf16(w_in_hbm_ref.at[layer], w_in_ref, stage_in_ref, sem_in_ref)
        _load_as_bf16(w_out_hbm_ref.at[layer], w_out_ref, stage_out_ref, sem_out_ref)

    shift = mod_ref[mod_row:mod_row + 1, :]
    scale = mod_ref[mod_row + 1:mod_row + 2, :]
    gate = mod_ref[mod_row + 2:mod_row + 3, :]
    for m0 in range(0, FFN_TILE, FFN_SUB):
        rows = slice(m0, m0 + FFN_SUB)
        x = x_ref[rows, :]
        h = _rms_mod(x, g_ref[0:1, :], shift, scale).astype(BF16)
        gu = _dot(h, w_in_ref[...])
        act = (_silu(gu[:, :D_FF]) * gu[:, D_FF:]).astype(BF16)
        y = x + (0.5 * gate) * _dot(act, w_out_ref[...])
        if final:
            y = _rms_mod(y, g_ref[1:2, :], mod_ref[N_MOD:N_MOD + 1, :], mod_ref[N_MOD + 1:N_MOD + 2, :])
        o_ref[rows, :] = y


def _ffn_call(x2d, mod, g, w_in_all, w_out_all, layer, mod_row, tiles_per_batch, final):
    n_tok, d = x2d.shape
    w_in_shape, w_out_shape = w_in_all.shape[1:], w_out_all.shape[1:]
    assert w_in_shape[0] % W_STAGE_CHUNKS == 0 and w_out_shape[0] % W_STAGE_CHUNKS == 0
    return pl.pallas_call(
        functools.partial(_ffn_kernel, layer=layer, mod_row=mod_row, final=final),
        grid=(n_tok // FFN_TILE,),
        in_specs=[
            pl.BlockSpec((FFN_TILE, d), lambda i: (i, 0)),
            pl.BlockSpec((None, N_MOD + 2, d), lambda i: (i // tiles_per_batch, 0, 0)),
            pl.BlockSpec((2, d), lambda i: (0, 0)),
            pl.BlockSpec(memory_space=pl.ANY),
            pl.BlockSpec(memory_space=pl.ANY),
        ],
        out_specs=pl.BlockSpec((FFN_TILE, d), lambda i: (i, 0)),
        out_shape=jax.ShapeDtypeStruct((n_tok, d), F32),
        scratch_shapes=[
            pltpu.VMEM(w_in_shape, BF16),
            pltpu.VMEM(w_out_shape, BF16),
            pltpu.VMEM((2, w_in_shape[0] // W_STAGE_CHUNKS, w_in_shape[1]), F32),
            pltpu.VMEM((2, w_out_shape[0] // W_STAGE_CHUNKS, w_out_shape[1]), F32),
            pltpu.SemaphoreType.DMA((2,)),
            pltpu.SemaphoreType.DMA((2,)),
        ],
        compiler_params=pltpu.CompilerParams(
            dimension_semantics=("arbitrary",), vmem_limit_bytes=VMEM_LIMIT_BYTES),
        name="ffn",
    )(x2d, mod, g, w_in_all, w_out_all)


def _conv_group(z, ubuf_ref, ushift_ref, wdw_ref, bdw_ref, gln_ref, bln_ref):
    tile = z.shape[0]
    ubuf_ref[CONV_PAD:CONV_PAD + tile, :] = z[:, :D_CONV] * _sigmoid(z[:, D_CONV:2 * D_CONV])
    first = CONV_PAD - (CONV_WIDTH - 1)
    shifted_rows = ushift_ref.shape[1]
    for s in range(1, SUBLANES):
        ushift_ref[s - 1] = ubuf_ref[s:s + shifted_rows, :]
    blocks = []
    for r0 in range(0, tile, CONV_ROWS):
        acc = jnp.zeros((CONV_ROWS, D_CONV), F32)
        for k in range(CONV_WIDTH):
            base, s = (first + k) // SUBLANES * SUBLANES, (first + k) % SUBLANES
            rows = slice(r0 + base, r0 + base + CONV_ROWS)
            tap = ubuf_ref[rows, :] if s == 0 else ushift_ref[s - 1, rows, :]
            acc = acc + wdw_ref[k:k + 1, :] * tap
        blocks.append(acc)
    y = jnp.concatenate(blocks, axis=0) + bdw_ref[...]
    ubuf_ref[0:CONV_PAD, :] = ubuf_ref[tile:tile + CONV_PAD, :]
    yc = y - jnp.mean(y, axis=-1, keepdims=True)
    yn = yc * lax.rsqrt(jnp.mean(yc * yc, axis=-1, keepdims=True) + EPS)
    return _silu(yn * gln_ref[...] + bln_ref[...])


def _gla_subtile(q, k, v, log_a, s_ref, gnorm_ref):
    t = GLA_TILE
    row = lax.broadcasted_iota(jnp.int32, (t, t), 0)
    col = lax.broadcasted_iota(jnp.int32, (t, t), 1)
    lower = col <= row
    tri = jnp.where(lower, 1.0, 0.0).astype(BF16)
    ones = jnp.ones((t, LANES), BF16)
    parts = _split_bf16(log_a, 2)
    cum = sum(_dot(tri, p) for p in parts)
    total_cols = sum(_dot_tn(p, ones) for p in parts)
    mid = cum[CHUNK - 1:CHUNK, :]
    end = cum[t - 1:t, :]
    rel = cum - mid
    e_pos = jnp.exp(rel)
    e_neg = jnp.exp(-rel)
    qs = q * (GLA_DK ** -0.5)
    q_fwd = (qs * e_pos).astype(BF16)
    q_bwd = (qs * e_neg).astype(BF16)
    q_in = (qs * jnp.exp(cum)).astype(BF16)
    k_fwd = (k * e_pos).astype(BF16)
    k_bwd = (k * e_neg).astype(BF16)
    k_out = (k * jnp.exp(end - cum)).astype(BF16)
    v16 = v.astype(BF16)

    same_chunk = (row >= CHUNK) == (col >= CHUNK)
    lane_head = lax.broadcasted_iota(jnp.int32, (t, D_QK), 1) // GLA_DK
    s_old = s_ref[...]
    s16 = s_old.astype(BF16)
    zero16 = jnp.zeros((), BF16)
    outs = []
    for h in range(GLA_HEADS):
        in_head = lane_head == h
        vh = v16[:, h * GLA_DV:(h + 1) * GLA_DV]
        att_fwd = _dot_nt(jnp.where(in_head, q_fwd, zero16), k_bwd)
        att_bwd = _dot_nt(jnp.where(in_head, q_bwd, zero16), k_fwd)
        att = jnp.where(lower, att_fwd, jnp.where(same_chunk, att_bwd, 0.0))
        o = _dot(att.astype(BF16), vh) + _dot(jnp.where(in_head, q_in, zero16), s16)
        o = o * lax.rsqrt(jnp.mean(o * o, axis=-1, keepdims=True) + EPS) * gnorm_ref[h:h + 1, :]
        outs.append(o)
    upd = _dot_tn(k_out, v16)
    decay = jnp.exp(total_cols)
    for h in range(GLA_HEADS):
        rows = slice(h * GLA_DK, (h + 1) * GLA_DK)
        s_ref[rows, :] = decay[rows, :] * s_old[rows, :] + upd[rows, h * GLA_DV:(h + 1) * GLA_DV]
    return jnp.concatenate(outs, axis=-1)


def _mixer_kernel(x_ref, mod_ref, g_ref, w_main_ref, w_glr_ref, wgu_ref, bg_ref, wdw_ref, bdw_ref,
                  gln_ref, bln_ref, gnorm_ref, w_out_ref, o_ref, s_ref, ubuf_ref, ushift_ref, mix_ref):
    @pl.when(pl.program_id(1) == 0)
    def _():
        s_ref[...] = jnp.zeros_like(s_ref)
        ubuf_ref[0:CONV_PAD, :] = jnp.zeros((CONV_PAD, D_CONV), F32)

    for m0 in range(0, MIX_TILE, MIX_SUB):
        sub = slice(m0, m0 + MIX_SUB)
        x = x_ref[sub, :]
        tile = MIX_SUB
        h = _rms_mod(x, g_ref[...], mod_ref[3:4, :], mod_ref[4:5, :]).astype(BF16)
        z = _dot(h, w_main_ref[...])
        glr = _dot(h, w_glr_ref[...])
        y_conv = _dot(_conv_group(z, ubuf_ref, ushift_ref, wdw_ref, bdw_ref, gln_ref, bln_ref).astype(BF16),
                      w_out_ref[:D_CONV, :])

        log_a = _log_sigmoid(_dot(glr.astype(BF16), wgu_ref[...]) + bg_ref[...]) * (1.0 / GATE_TAU)
        q0, k0, v0, r0 = 2 * D_CONV, 2 * D_CONV + D_QK, 2 * D_CONV + 2 * D_QK, 2 * D_CONV + 2 * D_QK + D_GLA
        for t0 in range(0, tile, GLA_TILE):
            rows = slice(t0, t0 + GLA_TILE)
            o = _gla_subtile(z[rows, q0:k0], z[rows, k0:v0], z[rows, v0:r0], log_a[rows, :], s_ref, gnorm_ref)
            mix_ref[rows, :] = (o * _silu(z[rows, r0:])).astype(BF16)

        y = y_conv + _dot(mix_ref[...], w_out_ref[D_CONV:, :])
        o_ref[sub, :] = x + mod_ref[5:6, :] * y


def _mixer_call(x, mod, g, w_main, w_glr, wgu, bg, wdw, bdw, gln, bln, gnorm, w_out):
    bsz, seq, d = x.shape
    resident = pl.Buffered(1)

    def whole(a):
        return pl.BlockSpec(a.shape, lambda b, t: (0,) * a.ndim, pipeline_mode=resident)

    return pl.pallas_call(
        _mixer_kernel,
        grid=(bsz, seq // MIX_TILE),
        in_specs=[
            pl.BlockSpec((None, MIX_TILE, d), lambda b, t: (b, t, 0)),
            pl.BlockSpec((None, N_MOD, d), lambda b, t: (b, 0, 0)),
            whole(g), whole(w_main), whole(w_glr), whole(wgu), whole(bg), whole(wdw), whole(bdw),
            whole(gln), whole(bln), whole(gnorm), whole(w_out),
        ],
        out_specs=pl.BlockSpec((None, MIX_TILE, d), lambda b, t: (b, t, 0)),
        out_shape=jax.ShapeDtypeStruct((bsz, seq, d), F32),
        scratch_shapes=[
            pltpu.VMEM((D_QK, GLA_DV), F32),
            pltpu.VMEM((CONV_PAD + MIX_SUB, D_CONV), F32),
            pltpu.VMEM((SUBLANES - 1, MIX_SUB + CONV_PAD - SUBLANES, D_CONV), F32),
            pltpu.VMEM((MIX_SUB, D_GLA), BF16),
        ],
        compiler_params=pltpu.CompilerParams(
            dimension_semantics=("arbitrary", "arbitrary"), vmem_limit_bytes=VMEM_LIMIT_BYTES),
        name="mixer",
    )(x, mod, g, w_main, w_glr, wgu, bg, wdw, bdw, gln, bln, gnorm, w_out)


def kernel(x, c, w_ada, b_ada, g_norm_ffn1, w_ffn1_in, w_ffn1_out, g_norm_mix, w_in, w_dw, b_dw, g_conv_ln, b_conv_ln, w_gate_up, b_gate, g_gla_norm, w_out, g_norm_ffn2, w_ffn2_in, w_ffn2_out, g_norm_final, w_ada_final, b_ada_final):
    bsz, seq, d = x.shape
    depth = w_ada.shape[0]
    assert seq % MIX_TILE == 0 and seq % FFN_TILE == 0 and FFN_TILE % FFN_SUB == 0 and MIX_TILE % MIX_SUB == 0
    assert MIX_SUB % GLA_TILE == 0 and MIX_SUB % CONV_ROWS == 0
    assert bsz <= SUBLANES
    tiles_per_batch = seq // FFN_TILE

    c_pad = jnp.pad(c, ((0, SUBLANES - bsz), (0, 0)))
    mod = _ada_call(c_pad, w_ada, b_ada[:, None, :], 1024)[:, :bsz].reshape(depth, bsz, N_MOD, d)
    fmod = _ada_call(c_pad, w_ada_final[None], b_ada_final[None, None, :], 1024)[0, :bsz].reshape(bsz, 2, d)

    w_glr = jnp.pad(w_in[:, :, D_MAIN:], ((0, 0), (0, 0), (0, LANES - GATE_RANK))).astype(BF16)
    wgu = jnp.pad(w_gate_up, ((0, 0), (0, LANES - GATE_RANK), (0, 0))).astype(BF16)

    x2d = x.reshape(bsz * seq, d)
    for l in range(depth):
        mod_l = jnp.concatenate([mod[l], fmod], axis=1)
        x2d = _ffn_call(x2d, mod_l, jnp.stack([g_norm_ffn1[l], g_norm_final]), w_ffn1_in,
                        w_ffn1_out, l, 0, tiles_per_batch, False)
        x3d = _mixer_call(
            x2d.reshape(bsz, seq, d), mod[l], g_norm_mix[l][None], w_in[l, :, :D_MAIN].astype(BF16), w_glr[l],
            wgu[l], b_gate[l][None], w_dw[l], b_dw[l][None], g_conv_ln[l][None], b_conv_ln[l][None],
            g_gla_norm[l], w_out[l].astype(BF16))
        x2d = _ffn_call(x3d.reshape(bsz * seq, d), mod_l, jnp.stack([g_norm_ffn2[l], g_norm_final]),
                        w_ffn2_in, w_ffn2_out, l, 6, tiles_per_batch, l == depth - 1)
    return x2d.reshape(bsz, seq, d)
```

```python
import functools

import jax
import jax.numpy as jnp
from jax import lax
from jax.experimental import pallas as pl
from jax.experimental.pallas import tpu as pltpu

D_MODEL = 1024
D_FF = 2816
D_CONV = 512
CONV_WIDTH = 31
GLA_HEADS = 4
GLA_DK = 64
GLA_DV = 128
D_QK = GLA_HEADS * GLA_DK
D_GLA = GLA_HEADS * GLA_DV
D_MIX = D_CONV + D_GLA
GATE_RANK = 16
GATE_TAU = 16.0
CHUNK = 64
N_MOD = 9
EPS = 1e-6
D_MAIN = 2 * D_CONV + 2 * D_QK + 2 * D_GLA

LANES = 128
SUBLANES = 8
VMEM_LIMIT_BYTES = 56 * 1024 * 1024

FFN_TILE = 1024
FFN_SUB = 256
MIX_TILE = 512
MIX_SUB = 256
GLA_TILE = 2 * CHUNK
CONV_ROWS = 32
CONV_PAD = 32
W_STAGE_CHUNKS = 8

F32 = jnp.float32
BF16 = jnp.bfloat16


def _dot(a, b):
    return jnp.dot(a, b, preferred_element_type=F32)


def _dot_nt(a, b):
    return lax.dot_general(a, b, (((1,), (1,)), ((), ())), preferred_element_type=F32)


def _dot_tn(a, b):
    return lax.dot_general(a, b, (((0,), (0,)), ((), ())), preferred_element_type=F32)


def _sigmoid(x):
    return 1.0 / (1.0 + jnp.exp(-x))


def _silu(x):
    return x * _sigmoid(x)


def _log_sigmoid(x):
    return jnp.minimum(x, 0.0) - jnp.log(1.0 + jnp.exp(-jnp.abs(x)))


def _rms_mod(x, g, shift, scale):
    y = x * lax.rsqrt(jnp.mean(x * x, axis=-1, keepdims=True) + EPS)
    return (y * g) * (1.0 + scale) + shift


def _split_bf16(x, parts):
    out = []
    for _ in range(parts):
        p = x.astype(BF16)
        out.append(p)
        x = x - p.astype(F32)
    return out


def _ada_kernel(c_ref, w_ref, b_ref, o_ref):
    c = _silu(c_ref[...])
    o_ref[...] = _dot(c.astype(BF16), w_ref[...].astype(BF16)) + b_ref[...]


def _ada_call(c_pad, w, b, tn):
    n_layers, d, n = w.shape
    return pl.pallas_call(
        _ada_kernel,
        grid=(n_layers, n // tn),
        in_specs=[
            pl.BlockSpec((SUBLANES, d), lambda l, j: (0, 0)),
            pl.BlockSpec((None, d, tn), lambda l, j: (l, 0, j)),
            pl.BlockSpec((None, 1, tn), lambda l, j: (l, 0, j)),
        ],
        out_specs=pl.BlockSpec((None, SUBLANES, tn), lambda l, j: (l, 0, j)),
        out_shape=jax.ShapeDtypeStruct((n_layers, SUBLANES, n), F32),
        compiler_params=pltpu.CompilerParams(
            dimension_semantics=("arbitrary", "arbitrary"), vmem_limit_bytes=VMEM_LIMIT_BYTES),
        name="ada",
    )(c_pad, w, b)


def _load_as_bf16(src_hbm_ref, stage_ref, sem_ref, store):
    rows = stage_ref.shape[1]
    n_chunks = src_hbm_ref.shape[0] // rows
    assert n_chunks * rows == src_hbm_ref.shape[0]

    def chunk_copy(c):
        return pltpu.make_async_copy(src_hbm_ref.at[pl.ds(c * rows, rows), :], stage_ref.at[c % 2], sem_ref.at[c % 2])

    chunk_copy(0).start()
    for c in range(n_chunks):
        if c + 1 < n_chunks:
            chunk_copy(c + 1).start()
        chunk_copy(c).wait()
        store(c * rows, stage_ref[c % 2].astype(BF16))


def _store_rows(dst_ref):
    def store(row0, chunk):
        dst_ref[row0:row0 + chunk.shape[0], :] = chunk
    return store


def _ffn_kernel(x_ref, mod_ref, g_ref, w_in_hbm_ref, w_out_hbm_ref, o_ref,
                w_in_ref, w_out_ref, stage_in_ref, stage_out_ref, sem_in_ref, sem_out_ref, *, layer, mod_row, final):
    @pl.when(pl.program_id(0) == 0)
    def _():
        _load_as_bf16(w_in_hbm_ref.at[layer], stage_in_ref, sem_in_ref, _store_rows(w_in_ref))
        _load_as_bf16(w_out_hbm_ref.at[layer], stage_out_ref, sem_out_ref, _store_rows(w_out_ref))

    shift = mod_ref[mod_row:mod_row + 1, :]
    scale = mod_ref[mod_row + 1:mod_row + 2, :]
    gate = mod_ref[mod_row + 2:mod_row + 3, :]
    for m0 in range(0, FFN_TILE, FFN_SUB):
        rows = slice(m0, m0 + FFN_SUB)
        x = x_ref[rows, :]
        h = _rms_mod(x, g_ref[0:1, :], shift, scale).astype(BF16)
        gu = _dot(h, w_in_ref[...])
        act = (_silu(gu[:, :D_FF]) * gu[:, D_FF:]).astype(BF16)
        y = x + (0.5 * gate) * _dot(act, w_out_ref[...])
        if final:
            y = _rms_mod(y, g_ref[1:2, :], mod_ref[N_MOD:N_MOD + 1, :], mod_ref[N_MOD + 1:N_MOD + 2, :])
        o_ref[rows, :] = y


def _ffn_call(x2d, mod, g, w_in_all, w_out_all, layer, mod_row, tiles_per_batch, final):
    n_tok, d = x2d.shape
    w_in_shape, w_out_shape = w_in_all.shape[1:], w_out_all.shape[1:]
    assert w_in_shape[0] % W_STAGE_CHUNKS == 0 and w_out_shape[0] % W_STAGE_CHUNKS == 0
    return pl.pallas_call(
        functools.partial(_ffn_kernel, layer=layer, mod_row=mod_row, final=final),
        grid=(n_tok // FFN_TILE,),
        in_specs=[
            pl.BlockSpec((FFN_TILE, d), lambda i: (i, 0)),
            pl.BlockSpec((None, N_MOD + 2, d), lambda i: (i // tiles_per_batch, 0, 0)),
            pl.BlockSpec((2, d), lambda i: (0, 0)),
            pl.BlockSpec(memory_space=pl.ANY),
            pl.BlockSpec(memory_space=pl.ANY),
        ],
        out_specs=pl.BlockSpec((FFN_TILE, d), lambda i: (i, 0)),
        out_shape=jax.ShapeDtypeStruct((n_tok, d), F32),
        scratch_shapes=[
            pltpu.VMEM(w_in_shape, BF16),
            pltpu.VMEM(w_out_shape, BF16),
            pltpu.VMEM((2, w_in_shape[0] // W_STAGE_CHUNKS, w_in_shape[1]), F32),
            pltpu.VMEM((2, w_out_shape[0] // W_STAGE_CHUNKS, w_out_shape[1]), F32),
            pltpu.SemaphoreType.DMA((2,)),
            pltpu.SemaphoreType.DMA((2,)),
        ],
        compiler_params=pltpu.CompilerParams(
            dimension_semantics=("arbitrary",), vmem_limit_bytes=VMEM_LIMIT_BYTES),
        name="ffn",
    )(x2d, mod, g, w_in_all, w_out_all)


def _conv_group(z, ubuf_ref, ushift_ref, wdw_ref, bdw_ref, gln_ref, bln_ref):
    tile = z.shape[0]
    ubuf_ref[CONV_PAD:CONV_PAD + tile, :] = z[:, :D_CONV] * _sigmoid(z[:, D_CONV:2 * D_CONV])
    first = CONV_PAD - (CONV_WIDTH - 1)
    shifted_rows = ushift_ref.shape[1]
    for s in range(1, SUBLANES):
        ushift_ref[s - 1] = ubuf_ref[s:s + shifted_rows, :]
    blocks = []
    for r0 in range(0, tile, CONV_ROWS):
        acc = jnp.zeros((CONV_ROWS, D_CONV), F32)
        for k in range(CONV_WIDTH):
            base, s = (first + k) // SUBLANES * SUBLANES, (first + k) % SUBLANES
            rows = slice(r0 + base, r0 + base + CONV_ROWS)
            tap = ubuf_ref[rows, :] if s == 0 else ushift_ref[s - 1, rows, :]
            acc = acc + wdw_ref[k:k + 1, :] * tap
        blocks.append(acc)
    y = jnp.concatenate(blocks, axis=0) + bdw_ref[...]
    ubuf_ref[0:CONV_PAD, :] = ubuf_ref[tile:tile + CONV_PAD, :]
    yc = y - jnp.mean(y, axis=-1, keepdims=True)
    yn = yc * lax.rsqrt(jnp.mean(yc * yc, axis=-1, keepdims=True) + EPS)
    return _silu(yn * gln_ref[...] + bln_ref[...])


def _gla_subtile(q, k, v, log_a, s_ref, gnorm_ref):
    t = GLA_TILE
    row = lax.broadcasted_iota(jnp.int32, (t, t), 0)
    col = lax.broadcasted_iota(jnp.int32, (t, t), 1)
    lower = col <= row
    tri = jnp.where(lower, 1.0, 0.0).astype(BF16)
    ones = jnp.ones((t, LANES), BF16)
    parts = _split_bf16(log_a, 2)
    cum = sum(_dot(tri, p) for p in parts)
    total_cols = sum(_dot_tn(p, ones) for p in parts)
    mid = cum[CHUNK - 1:CHUNK, :]
    end = cum[t - 1:t, :]
    rel = cum - mid
    e_pos = jnp.exp(rel)
    e_neg = jnp.exp(-rel)
    qs = q * (GLA_DK ** -0.5)
    q_fwd = (qs * e_pos).astype(BF16)
    q_bwd = (qs * e_neg).astype(BF16)
    q_in = (qs * jnp.exp(cum)).astype(BF16)
    k_fwd = (k * e_pos).astype(BF16)
    k_bwd = (k * e_neg).astype(BF16)
    k_out = (k * jnp.exp(end - cum)).astype(BF16)
    v16 = v.astype(BF16)

    same_chunk = (row >= CHUNK) == (col >= CHUNK)
    lane_head = lax.broadcasted_iota(jnp.int32, (t, D_QK), 1) // GLA_DK
    s_old = s_ref[...]
    s16 = s_old.astype(BF16)
    zero16 = jnp.zeros((), BF16)
    outs = []
    for h in range(GLA_HEADS):
        in_head = lane_head == h
        vh = v16[:, h * GLA_DV:(h + 1) * GLA_DV]
        att_fwd = _dot_nt(jnp.where(in_head, q_fwd, zero16), k_bwd)
        att_bwd = _dot_nt(jnp.where(in_head, q_bwd, zero16), k_fwd)
        att = jnp.where(lower, att_fwd, jnp.where(same_chunk, att_bwd, 0.0))
        o = _dot(att.astype(BF16), vh) + _dot(jnp.where(in_head, q_in, zero16), s16)
        o = o * lax.rsqrt(jnp.mean(o * o, axis=-1, keepdims=True) + EPS) * gnorm_ref[h:h + 1, :]
        outs.append(o)
    upd = _dot_tn(k_out, v16)
    decay = jnp.exp(total_cols)
    for h in range(GLA_HEADS):
        rows = slice(h * GLA_DK, (h + 1) * GLA_DK)
        s_ref[rows, :] = decay[rows, :] * s_old[rows, :] + upd[rows, h * GLA_DV:(h + 1) * GLA_DV]
    return jnp.concatenate(outs, axis=-1)


def _mixer_kernel(x_ref, mod_ref, g_ref, w_in_hbm_ref, wgu_ref, bg_ref, wdw_ref, bdw_ref, gln_ref, bln_ref, gnorm_ref,
                  w_out_hbm_ref, o_ref, s_ref, ubuf_ref, ushift_ref, mix_ref,
                  w_main_ref, w_glr_ref, w_out_ref, stage_in_ref, stage_out_ref, sem_in_ref, sem_out_ref, *, layer):
    @pl.when((pl.program_id(0) == 0) & (pl.program_id(1) == 0))
    def _():
        w_glr_ref[...] = jnp.zeros_like(w_glr_ref)

        def store_in(row0, chunk):
            rows = slice(row0, row0 + chunk.shape[0])
            w_main_ref[rows, :] = chunk[:, :D_MAIN]
            w_glr_ref[rows, :GATE_RANK] = chunk[:, D_MAIN:]

        _load_as_bf16(w_in_hbm_ref.at[layer], stage_in_ref, sem_in_ref, store_in)
        _load_as_bf16(w_out_hbm_ref.at[layer], stage_out_ref, sem_out_ref, _store_rows(w_out_ref))

    @pl.when(pl.program_id(1) == 0)
    def _():
        s_ref[...] = jnp.zeros_like(s_ref)
        ubuf_ref[0:CONV_PAD, :] = jnp.zeros((CONV_PAD, D_CONV), F32)

    for m0 in range(0, MIX_TILE, MIX_SUB):
        sub = slice(m0, m0 + MIX_SUB)
        x = x_ref[sub, :]
        tile = MIX_SUB
        h = _rms_mod(x, g_ref[...], mod_ref[3:4, :], mod_ref[4:5, :]).astype(BF16)
        z = _dot(h, w_main_ref[...])
        glr = _dot(h, w_glr_ref[...])
        y_conv = _dot(_conv_group(z, ubuf_ref, ushift_ref, wdw_ref, bdw_ref, gln_ref, bln_ref).astype(BF16),
                      w_out_ref[:D_CONV, :])

        log_a = _log_sigmoid(_dot(glr.astype(BF16), wgu_ref[...]) + bg_ref[...]) * (1.0 / GATE_TAU)
        q0, k0, v0, r0 = 2 * D_CONV, 2 * D_CONV + D_QK, 2 * D_CONV + 2 * D_QK, 2 * D_CONV + 2 * D_QK + D_GLA
        for t0 in range(0, tile, GLA_TILE):
            rows = slice(t0, t0 + GLA_TILE)
            o = _gla_subtile(z[rows, q0:k0], z[rows, k0:v0], z[rows, v0:r0], log_a[rows, :], s_ref, gnorm_ref)
            mix_ref[rows, :] = (o * _silu(z[rows, r0:])).astype(BF16)

        y = y_conv + _dot(mix_ref[...], w_out_ref[D_CONV:, :])
        o_ref[sub, :] = x + mod_ref[5:6, :] * y


def _mixer_call(x, mod, g, w_in_all, wgu, bg, wdw, bdw, gln, bln, gnorm, w_out_all, layer):
    bsz, seq, d = x.shape
    resident = pl.Buffered(1)
    d_in, d_mix = w_in_all.shape[2], w_out_all.shape[1]
    assert d_in == D_MAIN + GATE_RANK and d % W_STAGE_CHUNKS == 0 and d_mix % W_STAGE_CHUNKS == 0

    def whole(a):
        return pl.BlockSpec(a.shape, lambda b, t: (0,) * a.ndim, pipeline_mode=resident)

    hbm = pl.BlockSpec(memory_space=pl.ANY)
    return pl.pallas_call(
        functools.partial(_mixer_kernel, layer=layer),
        grid=(bsz, seq // MIX_TILE),
        in_specs=[
            pl.BlockSpec((None, MIX_TILE, d), lambda b, t: (b, t, 0)),
            pl.BlockSpec((None, N_MOD, d), lambda b, t: (b, 0, 0)),
            whole(g), hbm, whole(wgu), whole(bg), whole(wdw), whole(bdw),
            whole(gln), whole(bln), whole(gnorm), hbm,
        ],
        out_specs=pl.BlockSpec((None, MIX_TILE, d), lambda b, t: (b, t, 0)),
        out_shape=jax.ShapeDtypeStruct((bsz, seq, d), F32),
        scratch_shapes=[
            pltpu.VMEM((D_QK, GLA_DV), F32),
            pltpu.VMEM((CONV_PAD + MIX_SUB, D_CONV), F32),
            pltpu.VMEM((SUBLANES - 1, MIX_SUB + CONV_PAD - SUBLANES, D_CONV), F32),
            pltpu.VMEM((MIX_SUB, D_GLA), BF16),
            pltpu.VMEM((d, D_MAIN), BF16),
            pltpu.VMEM((d, LANES), BF16),
            pltpu.VMEM((d_mix, d), BF16),
            pltpu.VMEM((2, d // W_STAGE_CHUNKS, d_in), F32),
            pltpu.VMEM((2, d_mix // W_STAGE_CHUNKS, d), F32),
            pltpu.SemaphoreType.DMA((2,)),
            pltpu.SemaphoreType.DMA((2,)),
        ],
        compiler_params=pltpu.CompilerParams(
            dimension_semantics=("arbitrary", "arbitrary"), vmem_limit_bytes=VMEM_LIMIT_BYTES),
        name="mixer",
    )(x, mod, g, w_in_all, wgu, bg, wdw, bdw, gln, bln, gnorm, w_out_all)


def kernel(x, c, w_ada, b_ada, g_norm_ffn1, w_ffn1_in, w_ffn1_out, g_norm_mix, w_in, w_dw, b_dw, g_conv_ln, b_conv_ln, w_gate_up, b_gate, g_gla_norm, w_out, g_norm_ffn2, w_ffn2_in, w_ffn2_out, g_norm_final, w_ada_final, b_ada_final):
    bsz, seq, d = x.shape
    depth = w_ada.shape[0]
    assert seq % MIX_TILE == 0 and seq % FFN_TILE == 0 and FFN_TILE % FFN_SUB == 0 and MIX_TILE % MIX_SUB == 0
    assert MIX_SUB % GLA_TILE == 0 and MIX_SUB % CONV_ROWS == 0
    assert bsz <= SUBLANES
    tiles_per_batch = seq // FFN_TILE

    c_pad = jnp.pad(c, ((0, SUBLANES - bsz), (0, 0)))
    mod = _ada_call(c_pad, w_ada, b_ada[:, None, :], 1024)[:, :bsz].reshape(depth, bsz, N_MOD, d)
    fmod = _ada_call(c_pad, w_ada_final[None], b_ada_final[None, None, :], 1024)[0, :bsz].reshape(bsz, 2, d)

    wgu = jnp.pad(w_gate_up, ((0, 0), (0, LANES - GATE_RANK), (0, 0))).astype(BF16)

    x2d = x.reshape(bsz * seq, d)
    for l in range(depth):
        mod_l = jnp.concatenate([mod[l], fmod], axis=1)
        x2d = _ffn_call(x2d, mod_l, jnp.stack([g_norm_ffn1[l], g_norm_final]), w_ffn1_in,
                        w_ffn1_out, l, 0, tiles_per_batch, False)
        x3d = _mixer_call(
            x2d.reshape(bsz, seq, d), mod[l], g_norm_mix[l][None], w_in, wgu[l], b_gate[l][None], w_dw[l],
            b_dw[l][None], g_conv_ln[l][None], b_conv_ln[l][None], g_gla_norm[l], w_out, l)
        x2d = _ffn_call(x3d.reshape(bsz * seq, d), mod_l, jnp.stack([g_norm_ffn2[l], g_norm_final]),
                        w_ffn2_in, w_ffn2_out, l, 6, tiles_per_batch, l == depth - 1)
    return x2d.reshape(bsz, seq, d)
```

```python
import functools

import jax
import jax.numpy as jnp
from jax import lax
from jax.experimental import pallas as pl
from jax.experimental.pallas import tpu as pltpu

D_MODEL = 1024
D_FF = 2816
D_CONV = 512
CONV_WIDTH = 31
GLA_HEADS = 4
GLA_DK = 64
GLA_DV = 128
D_QK = GLA_HEADS * GLA_DK
D_GLA = GLA_HEADS * GLA_DV
D_MIX = D_CONV + D_GLA
GATE_RANK = 16
GATE_TAU = 16.0
CHUNK = 64
N_MOD = 9
EPS = 1e-6
D_MAIN = 2 * D_CONV + 2 * D_QK + 2 * D_GLA

LANES = 128
SUBLANES = 8
VMEM_LIMIT_BYTES = 56 * 1024 * 1024

FFN_TILE = 1024
FFN_SUB = 128
MIX_TILE = 512
MIX_SUB = 256
GLA_TILE = 2 * CHUNK
CONV_ROWS = 32
CONV_PAD = 32
W_STAGE_CHUNKS = 8

F32 = jnp.float32
BF16 = jnp.bfloat16


def _dot(a, b):
    return jnp.dot(a, b, preferred_element_type=F32)


def _dot_nt(a, b):
    return lax.dot_general(a, b, (((1,), (1,)), ((), ())), preferred_element_type=F32)


def _dot_tn(a, b):
    return lax.dot_general(a, b, (((0,), (0,)), ((), ())), preferred_element_type=F32)


def _sigmoid(x):
    return 1.0 / (1.0 + jnp.exp(-x))


def _silu(x):
    return x * _sigmoid(x)


def _log_sigmoid(x):
    return jnp.minimum(x, 0.0) - jnp.log(1.0 + jnp.exp(-jnp.abs(x)))


def _rms_mod(x, g, shift, scale):
    y = x * lax.rsqrt(jnp.mean(x * x, axis=-1, keepdims=True) + EPS)
    return (y * g) * (1.0 + scale) + shift


def _split_bf16(x, parts):
    out = []
    for _ in range(parts):
        p = x.astype(BF16)
        out.append(p)
        x = x - p.astype(F32)
    return out


def _ada_kernel(c_ref, w_ref, b_ref, o_ref):
    c = _silu(c_ref[...])
    o_ref[...] = _dot(c.astype(BF16), w_ref[...].astype(BF16)) + b_ref[...]


def _ada_call(c_pad, w, b, tn):
    n_layers, d, n = w.shape
    return pl.pallas_call(
        _ada_kernel,
        grid=(n_layers, n // tn),
        in_specs=[
            pl.BlockSpec((SUBLANES, d), lambda l, j: (0, 0)),
            pl.BlockSpec((None, d, tn), lambda l, j: (l, 0, j)),
            pl.BlockSpec((None, 1, tn), lambda l, j: (l, 0, j)),
        ],
        out_specs=pl.BlockSpec((None, SUBLANES, tn), lambda l, j: (l, 0, j)),
        out_shape=jax.ShapeDtypeStruct((n_layers, SUBLANES, n), F32),
        compiler_params=pltpu.CompilerParams(
            dimension_semantics=("arbitrary", "arbitrary"), vmem_limit_bytes=VMEM_LIMIT_BYTES),
        name="ada",
    )(c_pad, w, b)


def _load_as_bf16(src_hbm_ref, dst_ref, stage_ref, sem_ref):
    rows = stage_ref.shape[1]
    n_chunks = src_hbm_ref.shape[0] // rows
    assert n_chunks * rows == src_hbm_ref.shape[0] == dst_ref.shape[0]

    def chunk_copy(c):
        return pltpu.make_async_copy(src_hbm_ref.at[pl.ds(c * rows, rows), :], stage_ref.at[c % 2], sem_ref.at[c % 2])

    chunk_copy(0).start()
    for c in range(n_chunks):
        if c + 1 < n_chunks:
            chunk_copy(c + 1).start()
        chunk_copy(c).wait()
        dst_ref[c * rows:(c + 1) * rows, :] = stage_ref[c % 2].astype(BF16)


def _ffn_kernel(x_ref, mod_ref, g_ref, w_in_hbm_ref, w_out_hbm_ref, o_ref,
                w_in_ref, w_out_ref, stage_in_ref, stage_out_ref, sem_in_ref, sem_out_ref, *, layer, mod_row, final):
    @pl.when(pl.program_id(0) == 0)
    def _():
        _load_as_bf16(w_in_hbm_ref.at[layer], w_in_ref, stage_in_ref, sem_in_ref)
        _load_as_bf16(w_out_hbm_ref.at[layer], w_out_ref, stage_out_ref, sem_out_ref)

    shift = mod_ref[mod_row:mod_row + 1, :]
    scale = mod_ref[mod_row + 1:mod_row + 2, :]
    gate = mod_ref[mod_row + 2:mod_row + 3, :]
    for m0 in range(0, FFN_TILE, FFN_SUB):
        rows = slice(m0, m0 + FFN_SUB)
        x = x_ref[rows, :]
        h = _rms_mod(x, g_ref[0:1, :], shift, scale).astype(BF16)
        gu = _dot(h, w_in_ref[...])
        act = (_silu(gu[:, :D_FF]) * gu[:, D_FF:]).astype(BF16)
        y = x + (0.5 * gate) * _dot(act, w_out_ref[...])
        if final:
            y = _rms_mod(y, g_ref[1:2, :], mod_ref[N_MOD:N_MOD + 1, :], mod_ref[N_MOD + 1:N_MOD + 2, :])
        o_ref[rows, :] = y


def _ffn_call(x2d, mod, g, w_in_all, w_out_all, layer, mod_row, tiles_per_batch, final):
    n_tok, d = x2d.shape
    w_in_shape, w_out_shape = w_in_all.shape[1:], w_out_all.shape[1:]
    assert w_in_shape[0] % W_STAGE_CHUNKS == 0 and w_out_shape[0] % W_STAGE_CHUNKS == 0
    return pl.pallas_call(
        functools.partial(_ffn_kernel, layer=layer, mod_row=mod_row, final=final),
        grid=(n_tok // FFN_TILE,),
        in_specs=[
            pl.BlockSpec((FFN_TILE, d), lambda i: (i, 0)),
            pl.BlockSpec((None, N_MOD + 2, d), lambda i: (i // tiles_per_batch, 0, 0)),
            pl.BlockSpec((2, d), lambda i: (0, 0)),
            pl.BlockSpec(memory_space=pl.ANY),
            pl.BlockSpec(memory_space=pl.ANY),
        ],
        out_specs=pl.BlockSpec((FFN_TILE, d), lambda i: (i, 0)),
        out_shape=jax.ShapeDtypeStruct((n_tok, d), F32),
        scratch_shapes=[
            pltpu.VMEM(w_in_shape, BF16),
            pltpu.VMEM(w_out_shape, BF16),
            pltpu.VMEM((2, w_in_shape[0] // W_STAGE_CHUNKS, w_in_shape[1]), F32),
            pltpu.VMEM((2, w_out_shape[0] // W_STAGE_CHUNKS, w_out_shape[1]), F32),
            pltpu.SemaphoreType.DMA((2,)),
            pltpu.SemaphoreType.DMA((2,)),
        ],
        compiler_params=pltpu.CompilerParams(
            dimension_semantics=("arbitrary",), vmem_limit_bytes=VMEM_LIMIT_BYTES),
        name="ffn",
    )(x2d, mod, g, w_in_all, w_out_all)


def _conv_group(z, ubuf_ref, ushift_ref, wdw_ref, bdw_ref, gln_ref, bln_ref):
    tile = z.shape[0]
    ubuf_ref[CONV_PAD:CONV_PAD + tile, :] = z[:, :D_CONV] * _sigmoid(z[:, D_CONV:2 * D_CONV])
    first = CONV_PAD - (CONV_WIDTH - 1)
    shifted_rows = ushift_ref.shape[1]
    for s in range(1, SUBLANES):
        ushift_ref[s - 1] = ubuf_ref[s:s + shifted_rows, :]
    blocks = []
    for r0 in range(0, tile, CONV_ROWS):
        acc = jnp.zeros((CONV_ROWS, D_CONV), F32)
        for k in range(CONV_WIDTH):
            base, s = (first + k) // SUBLANES * SUBLANES, (first + k) % SUBLANES
            rows = slice(r0 + base, r0 + base + CONV_ROWS)
            tap = ubuf_ref[rows, :] if s == 0 else ushift_ref[s - 1, rows, :]
            acc = acc + wdw_ref[k:k + 1, :] * tap
        blocks.append(acc)
    y = jnp.concatenate(blocks, axis=0) + bdw_ref[...]
    ubuf_ref[0:CONV_PAD, :] = ubuf_ref[tile:tile + CONV_PAD, :]
    yc = y - jnp.mean(y, axis=-1, keepdims=True)
    yn = yc * lax.rsqrt(jnp.mean(yc * yc, axis=-1, keepdims=True) + EPS)
    return _silu(yn * gln_ref[...] + bln_ref[...])


def _gla_subtile(q, k, v, log_a, s_ref, gnorm_ref):
    t = GLA_TILE
    row = lax.broadcasted_iota(jnp.int32, (t, t), 0)
    col = lax.broadcasted_iota(jnp.int32, (t, t), 1)
    lower = col <= row
    tri = jnp.where(lower, 1.0, 0.0).astype(BF16)
    ones = jnp.ones((t, LANES), BF16)
    parts = _split_bf16(log_a, 2)
    cum = sum(_dot(tri, p) for p in parts)
    total_cols = sum(_dot_tn(p, ones) for p in parts)
    mid = cum[CHUNK - 1:CHUNK, :]
    end = cum[t - 1:t, :]
    rel = cum - mid
    e_pos = jnp.exp(rel)
    e_neg = jnp.exp(-rel)
    qs = q * (GLA_DK ** -0.5)
    q_fwd = (qs * e_pos).astype(BF16)
    q_bwd = (qs * e_neg).astype(BF16)
    q_in = (qs * jnp.exp(cum)).astype(BF16)
    k_fwd = (k * e_pos).astype(BF16)
    k_bwd = (k * e_neg).astype(BF16)
    k_out = (k * jnp.exp(end - cum)).astype(BF16)
    v16 = v.astype(BF16)

    same_chunk = (row >= CHUNK) == (col >= CHUNK)
    lane_head = lax.broadcasted_iota(jnp.int32, (t, D_QK), 1) // GLA_DK
    zero16 = jnp.zeros((), BF16)

    def per_head_keys(k16):
        return jnp.concatenate([jnp.where(lane_head == h, k16, zero16) for h in range(GLA_HEADS)], axis=0)

    att_fwd = _dot_nt(q_fwd, per_head_keys(k_bwd))
    att_bwd = _dot_nt(q_bwd, per_head_keys(k_fwd))
    s_old = s_ref[...]
    o_state = _dot(q_in, s_old.astype(BF16))
    outs = []
    for h in range(GLA_HEADS):
        cols = slice(h * t, (h + 1) * t)
        vh = v16[:, h * GLA_DV:(h + 1) * GLA_DV]
        att = jnp.where(lower, att_fwd[:, cols], jnp.where(same_chunk, att_bwd[:, cols], 0.0))
        o = _dot(att.astype(BF16), vh) + o_state[:, h * GLA_DV:(h + 1) * GLA_DV]
        o = o * lax.rsqrt(jnp.mean(o * o, axis=-1, keepdims=True) + EPS) * gnorm_ref[h:h + 1, :]
        outs.append(o)
    upd = _dot_tn(k_out, v16)
    decay = jnp.exp(total_cols)
    for h in range(GLA_HEADS):
        rows, cols = slice(h * GLA_DK, (h + 1) * GLA_DK), slice(h * GLA_DV, (h + 1) * GLA_DV)
        s_ref[rows, cols] = decay[rows, :] * s_old[rows, cols] + upd[rows, cols]
    return jnp.concatenate(outs, axis=-1)


def _mixer_kernel(x_ref, mod_ref, g_ref, w_main_ref, w_glr_ref, wgu_ref, bg_ref, wdw_ref, bdw_ref,
                  gln_ref, bln_ref, gnorm_ref, w_out_ref, o_ref, s_ref, ubuf_ref, ushift_ref, mix_ref):
    @pl.when(pl.program_id(1) == 0)
    def _():
        s_ref[...] = jnp.zeros_like(s_ref)
        ubuf_ref[0:CONV_PAD, :] = jnp.zeros((CONV_PAD, D_CONV), F32)

    for m0 in range(0, MIX_TILE, MIX_SUB):
        sub = slice(m0, m0 + MIX_SUB)
        x = x_ref[sub, :]
        tile = MIX_SUB
        h = _rms_mod(x, g_ref[...], mod_ref[3:4, :], mod_ref[4:5, :]).astype(BF16)
        z = _dot(h, w_main_ref[...])
        glr = _dot(h, w_glr_ref[...])
        y_conv = _dot(_conv_group(z, ubuf_ref, ushift_ref, wdw_ref, bdw_ref, gln_ref, bln_ref).astype(BF16),
                      w_out_ref[:D_CONV, :])

        log_a = _log_sigmoid(_dot(glr.astype(BF16), wgu_ref[...]) + bg_ref[...]) * (1.0 / GATE_TAU)
        q0, k0, v0, r0 = 2 * D_CONV, 2 * D_CONV + D_QK, 2 * D_CONV + 2 * D_QK, 2 * D_CONV + 2 * D_QK + D_GLA
        for t0 in range(0, tile, GLA_TILE):
            rows = slice(t0, t0 + GLA_TILE)
            o = _gla_subtile(z[rows, q0:k0], z[rows, k0:v0], z[rows, v0:r0], log_a[rows, :], s_ref, gnorm_ref)
            mix_ref[rows, :] = (o * _silu(z[rows, r0:])).astype(BF16)

        y = y_conv + _dot(mix_ref[...], w_out_ref[D_CONV:, :])
        o_ref[sub, :] = x + mod_ref[5:6, :] * y


def _mixer_call(x, mod, g, w_main, w_glr, wgu, bg, wdw, bdw, gln, bln, gnorm, w_out):
    bsz, seq, d = x.shape
    resident = pl.Buffered(1)

    def whole(a):
        return pl.BlockSpec(a.shape, lambda b, t: (0,) * a.ndim, pipeline_mode=resident)

    return pl.pallas_call(
        _mixer_kernel,
        grid=(bsz, seq // MIX_TILE),
        in_specs=[
            pl.BlockSpec((None, MIX_TILE, d), lambda b, t: (b, t, 0)),
            pl.BlockSpec((None, N_MOD, d), lambda b, t: (b, 0, 0)),
            whole(g), whole(w_main), whole(w_glr), whole(wgu), whole(bg), whole(wdw), whole(bdw),
            whole(gln), whole(bln), whole(gnorm), whole(w_out),
        ],
        out_specs=pl.BlockSpec((None, MIX_TILE, d), lambda b, t: (b, t, 0)),
        out_shape=jax.ShapeDtypeStruct((bsz, seq, d), F32),
        scratch_shapes=[
            pltpu.VMEM((D_QK, D_GLA), F32),
            pltpu.VMEM((CONV_PAD + MIX_SUB, D_CONV), F32),
            pltpu.VMEM((SUBLANES - 1, MIX_SUB + CONV_PAD - SUBLANES, D_CONV), F32),
            pltpu.VMEM((MIX_SUB, D_GLA), BF16),
        ],
        compiler_params=pltpu.CompilerParams(
            dimension_semantics=("arbitrary", "arbitrary"), vmem_limit_bytes=VMEM_LIMIT_BYTES),
        name="mixer",
    )(x, mod, g, w_main, w_glr, wgu, bg, wdw, bdw, gln, bln, gnorm, w_out)


def kernel(x, c, w_ada, b_ada, g_norm_ffn1, w_ffn1_in, w_ffn1_out, g_norm_mix, w_in, w_dw, b_dw, g_conv_ln, b_conv_ln, w_gate_up, b_gate, g_gla_norm, w_out, g_norm_ffn2, w_ffn2_in, w_ffn2_out, g_norm_final, w_ada_final, b_ada_final):
    bsz, seq, d = x.shape
    depth = w_ada.shape[0]
    assert seq % MIX_TILE == 0 and seq % FFN_TILE == 0 and FFN_TILE % FFN_SUB == 0 and MIX_TILE % MIX_SUB == 0
    assert MIX_SUB % GLA_TILE == 0 and MIX_SUB % CONV_ROWS == 0
    assert bsz <= SUBLANES
    tiles_per_batch = seq // FFN_TILE

    c_pad = jnp.pad(c, ((0, SUBLANES - bsz), (0, 0)))
    mod = _ada_call(c_pad, w_ada, b_ada[:, None, :], 1024)[:, :bsz].reshape(depth, bsz, N_MOD, d)
    fmod = _ada_call(c_pad, w_ada_final[None], b_ada_final[None, None, :], 1024)[0, :bsz].reshape(bsz, 2, d)

    w_glr = jnp.pad(w_in[:, :, D_MAIN:], ((0, 0), (0, 0), (0, LANES - GATE_RANK))).astype(BF16)
    wgu = jnp.pad(w_gate_up, ((0, 0), (0, LANES - GATE_RANK), (0, 0))).astype(BF16)

    x2d = x.reshape(bsz * seq, d)
    for l in range(depth):
        mod_l = jnp.concatenate([mod[l], fmod], axis=1)
        x2d = _ffn_call(x2d, mod_l, jnp.stack([g_norm_ffn1[l], g_norm_final]), w_ffn1_in,
                        w_ffn1_out, l, 0, tiles_per_batch, False)
        x3d = _mixer_call(
            x2d.reshape(bsz, seq, d), mod[l], g_norm_mix[l][None], w_in[l, :, :D_MAIN].astype(BF16), w_glr[l],
            wgu[l], b_gate[l][None], w_dw[l], b_dw[l][None], g_conv_ln[l][None], b_conv_ln[l][None],
            g_gla_norm[l], w_out[l].astype(BF16))
        x2d = _ffn_call(x3d.reshape(bsz * seq, d), mod_l, jnp.stack([g_norm_ffn2[l], g_norm_final]),
                        w_ffn2_in, w_ffn2_out, l, 6, tiles_per_batch, l == depth - 1)
    return x2d.reshape(bsz, seq, d)
```

```python
import functools

import jax
import jax.numpy as jnp
from jax import lax
from jax.experimental import pallas as pl
from jax.experimental.pallas import tpu as pltpu

D_MODEL = 1024
D_FF = 2816
D_CONV = 512
CONV_WIDTH = 31
GLA_HEADS = 4
GLA_DK = 64
GLA_DV = 128
D_QK = GLA_HEADS * GLA_DK
D_GLA = GLA_HEADS * GLA_DV
D_MIX = D_CONV + D_GLA
GATE_RANK = 16
GATE_TAU = 16.0
CHUNK = 64
N_MOD = 9
EPS = 1e-6
D_MAIN = 2 * D_CONV + 2 * D_QK + 2 * D_GLA

LANES = 128
SUBLANES = 8
VMEM_LIMIT_BYTES = 56 * 1024 * 1024

FFN_TILE = 1024
FFN_SUB = 128
MIX_TILE = 512
MIX_SUB = 256
GLA_TILE = 2 * CHUNK
CONV_ROWS = 32
CONV_PAD = 32
W_STAGE_CHUNKS = 8

F32 = jnp.float32
BF16 = jnp.bfloat16


def _dot(a, b):
    return jnp.dot(a, b, preferred_element_type=F32)


def _dot_nt(a, b):
    return lax.dot_general(a, b, (((1,), (1,)), ((), ())), preferred_element_type=F32)


def _dot_tn(a, b):
    return lax.dot_general(a, b, (((0,), (0,)), ((), ())), preferred_element_type=F32)


def _sigmoid(x):
    return 1.0 / (1.0 + jnp.exp(-x))


def _silu(x):
    return x * _sigmoid(x)


def _log_sigmoid(x):
    return jnp.minimum(x, 0.0) - jnp.log(1.0 + jnp.exp(-jnp.abs(x)))


def _rms_mod(x, g, shift, scale):
    y = x * lax.rsqrt(jnp.mean(x * x, axis=-1, keepdims=True) + EPS)
    return (y * g) * (1.0 + scale) + shift


def _split_bf16(x, parts):
    out = []
    for _ in range(parts):
        p = x.astype(BF16)
        out.append(p)
        x = x - p.astype(F32)
    return out


def _ada_kernel(c_ref, w_ref, b_ref, o_ref):
    c = _silu(c_ref[...])
    o_ref[...] = _dot(c.astype(BF16), w_ref[...].astype(BF16)) + b_ref[...]


def _ada_call(c_pad, w, b, tn):
    n_layers, d, n = w.shape
    return pl.pallas_call(
        _ada_kernel,
        grid=(n_layers, n // tn),
        in_specs=[
            pl.BlockSpec((SUBLANES, d), lambda l, j: (0, 0)),
            pl.BlockSpec((None, d, tn), lambda l, j: (l, 0, j)),
            pl.BlockSpec((None, 1, tn), lambda l, j: (l, 0, j)),
        ],
        out_specs=pl.BlockSpec((None, SUBLANES, tn), lambda l, j: (l, 0, j)),
        out_shape=jax.ShapeDtypeStruct((n_layers, SUBLANES, n), F32),
        compiler_params=pltpu.CompilerParams(
            dimension_semantics=("arbitrary", "arbitrary"), vmem_limit_bytes=VMEM_LIMIT_BYTES),
        name="ada",
    )(c_pad, w, b)


def _load_as_bf16(src_hbm_ref, dst_ref, stage_ref, sem_ref):
    rows = stage_ref.shape[1]
    n_chunks = src_hbm_ref.shape[0] // rows
    assert n_chunks * rows == src_hbm_ref.shape[0] == dst_ref.shape[0]

    def chunk_copy(c):
        return pltpu.make_async_copy(src_hbm_ref.at[pl.ds(c * rows, rows), :], stage_ref.at[c % 2], sem_ref.at[c % 2])

    chunk_copy(0).start()
    for c in range(n_chunks):
        if c + 1 < n_chunks:
            chunk_copy(c + 1).start()
        chunk_copy(c).wait()
        dst_ref[c * rows:(c + 1) * rows, :] = stage_ref[c % 2].astype(BF16)


def _ffn_kernel(x_ref, mod_ref, g_ref, w_in_hbm_ref, w_out_hbm_ref, o_ref,
                w_in_ref, w_out_ref, stage_in_ref, stage_out_ref, sem_in_ref, sem_out_ref, *, layer, mod_row, final):
    @pl.when(pl.program_id(0) == 0)
    def _():
        _load_as_bf16(w_in_hbm_ref.at[layer], w_in_ref, stage_in_ref, sem_in_ref)
        _load_as_bf16(w_out_hbm_ref.at[layer], w_out_ref, stage_out_ref, sem_out_ref)

    shift = mod_ref[mod_row:mod_row + 1, :]
    scale = mod_ref[mod_row + 1:mod_row + 2, :]
    gate = mod_ref[mod_row + 2:mod_row + 3, :]
    for m0 in range(0, FFN_TILE, FFN_SUB):
        rows = slice(m0, m0 + FFN_SUB)
        x = x_ref[rows, :]
        h = _rms_mod(x, g_ref[0:1, :], shift, scale).astype(BF16)
        gu = _dot(h, w_in_ref[...])
        act = (_silu(gu[:, :D_FF]) * gu[:, D_FF:]).astype(BF16)
        y = x + (0.5 * gate) * _dot(act, w_out_ref[...])
        if final:
            y = _rms_mod(y, g_ref[1:2, :], mod_ref[N_MOD:N_MOD + 1, :], mod_ref[N_MOD + 1:N_MOD + 2, :])
        o_ref[rows, :] = y


def _ffn_call(x2d, mod, g, w_in_all, w_out_all, layer, mod_row, tiles_per_batch, final):
    n_tok, d = x2d.shape
    w_in_shape, w_out_shape = w_in_all.shape[1:], w_out_all.shape[1:]
    assert w_in_shape[0] % W_STAGE_CHUNKS == 0 and w_out_shape[0] % W_STAGE_CHUNKS == 0
    return pl.pallas_call(
        functools.partial(_ffn_kernel, layer=layer, mod_row=mod_row, final=final),
        grid=(n_tok // FFN_TILE,),
        in_specs=[
            pl.BlockSpec((FFN_TILE, d), lambda i: (i, 0)),
            pl.BlockSpec((None, N_MOD + 2, d), lambda i: (i // tiles_per_batch, 0, 0)),
            pl.BlockSpec((2, d), lambda i: (0, 0)),
            pl.BlockSpec(memory_space=pl.ANY),
            pl.BlockSpec(memory_space=pl.ANY),
        ],
        out_specs=pl.BlockSpec((FFN_TILE, d), lambda i: (i, 0)),
        out_shape=jax.ShapeDtypeStruct((n_tok, d), F32),
        scratch_shapes=[
            pltpu.VMEM(w_in_shape, BF16),
            pltpu.VMEM(w_out_shape, BF16),
            pltpu.VMEM((2, w_in_shape[0] // W_STAGE_CHUNKS, w_in_shape[1]), F32),
            pltpu.VMEM((2, w_out_shape[0] // W_STAGE_CHUNKS, w_out_shape[1]), F32),
            pltpu.SemaphoreType.DMA((2,)),
            pltpu.SemaphoreType.DMA((2,)),
        ],
        compiler_params=pltpu.CompilerParams(
            dimension_semantics=("arbitrary",), vmem_limit_bytes=VMEM_LIMIT_BYTES),
        name="ffn",
    )(x2d, mod, g, w_in_all, w_out_all)


def _conv_group(z, ubuf_ref, ushift_ref, wdw_ref, bdw_ref, gln_ref, bln_ref):
    tile = z.shape[0]
    ubuf_ref[CONV_PAD:CONV_PAD + tile, :] = z[:, :D_CONV] * _sigmoid(z[:, D_CONV:2 * D_CONV])
    first = CONV_PAD - (CONV_WIDTH - 1)
    shifted_rows = ushift_ref.shape[1]
    for s in range(1, SUBLANES):
        ushift_ref[s - 1] = ubuf_ref[s:s + shifted_rows, :]
    blocks = []
    for r0 in range(0, tile, CONV_ROWS):
        acc = jnp.zeros((CONV_ROWS, D_CONV), F32)
        for k in range(CONV_WIDTH):
            base, s = (first + k) // SUBLANES * SUBLANES, (first + k) % SUBLANES
            rows = slice(r0 + base, r0 + base + CONV_ROWS)
            tap = ubuf_ref[rows, :] if s == 0 else ushift_ref[s - 1, rows, :]
            acc = acc + wdw_ref[k:k + 1, :] * tap
        blocks.append(acc)
    y = jnp.concatenate(blocks, axis=0) + bdw_ref[...]
    ubuf_ref[0:CONV_PAD, :] = ubuf_ref[tile:tile + CONV_PAD, :]
    yc = y - jnp.mean(y, axis=-1, keepdims=True)
    yn = yc * lax.rsqrt(jnp.mean(yc * yc, axis=-1, keepdims=True) + EPS)
    return _silu(yn * gln_ref[...] + bln_ref[...])


def _gla_subtile(q, k, v, log_a, s_ref, gnorm_ref):
    t = GLA_TILE
    row = lax.broadcasted_iota(jnp.int32, (t, t), 0)
    col = lax.broadcasted_iota(jnp.int32, (t, t), 1)
    lower = col <= row
    tri = jnp.where(lower, 1.0, 0.0).astype(BF16)
    ones = jnp.ones((t, LANES), BF16)
    parts = _split_bf16(log_a, 2)
    cum = sum(_dot(tri, p) for p in parts)
    total_cols = sum(_dot_tn(p, ones) for p in parts)
    mid = cum[CHUNK - 1:CHUNK, :]
    end = cum[t - 1:t, :]
    rel = cum - mid
    e_pos = jnp.exp(rel)
    e_neg = jnp.exp(-rel)
    qs = q * (GLA_DK ** -0.5)
    q_fwd = (qs * e_pos).astype(BF16)
    q_bwd = (qs * e_neg).astype(BF16)
    q_in = (qs * jnp.exp(cum)).astype(BF16)
    k_fwd = (k * e_pos).astype(BF16)
    k_bwd = (k * e_neg).astype(BF16)
    k_out = (k * jnp.exp(end - cum)).astype(BF16)
    v16 = v.astype(BF16)

    same_chunk = (row >= CHUNK) == (col >= CHUNK)
    lane_head = lax.broadcasted_iota(jnp.int32, (t, D_QK), 1) // GLA_DK
    zero16 = jnp.zeros((), BF16)

    def per_head_keys(k16):
        return jnp.concatenate([jnp.where(lane_head == h, k16, zero16) for h in range(GLA_HEADS)], axis=0)

    att_fwd = _dot_nt(q_fwd, per_head_keys(k_bwd))
    att_bwd = _dot_nt(q_bwd, per_head_keys(k_fwd))
    s_old = s_ref[...]
    o_state = _dot(q_in, s_old.astype(BF16))
    outs = []
    for h in range(GLA_HEADS):
        cols = slice(h * t, (h + 1) * t)
        vh = v16[:, h * GLA_DV:(h + 1) * GLA_DV]
        att = jnp.where(lower, att_fwd[:, cols], jnp.where(same_chunk, att_bwd[:, cols], 0.0))
        o = _dot(att.astype(BF16), vh) + o_state[:, h * GLA_DV:(h + 1) * GLA_DV]
        o = o * lax.rsqrt(jnp.mean(o * o, axis=-1, keepdims=True) + EPS) * gnorm_ref[h:h + 1, :]
        outs.append(o)
    upd = _dot_tn(k_out, v16)
    decay = jnp.exp(total_cols)
    for h in range(GLA_HEADS):
        rows, cols = slice(h * GLA_DK, (h + 1) * GLA_DK), slice(h * GLA_DV, (h + 1) * GLA_DV)
        s_ref[rows, cols] = decay[rows, :] * s_old[rows, cols] + upd[rows, cols]
    return jnp.concatenate(outs, axis=-1)


def _mixer_kernel(x_ref, mod_ref, g_ref, w_in_ref, wgu_ref, bg_ref, wdw_ref, bdw_ref,
                  gln_ref, bln_ref, gnorm_ref, w_out_ref, o_ref, s_ref, ubuf_ref, ushift_ref, mix_ref):
    @pl.when(pl.program_id(1) == 0)
    def _():
        s_ref[...] = jnp.zeros_like(s_ref)
        ubuf_ref[0:CONV_PAD, :] = jnp.zeros((CONV_PAD, D_CONV), F32)

    for m0 in range(0, MIX_TILE, MIX_SUB):
        sub = slice(m0, m0 + MIX_SUB)
        x = x_ref[sub, :]
        tile = MIX_SUB
        h = _rms_mod(x, g_ref[...], mod_ref[3:4, :], mod_ref[4:5, :]).astype(BF16)
        z = _dot(h, w_in_ref[:, :D_MAIN])
        glr = _dot(h, w_in_ref[:, D_MAIN:])
        y_conv = _dot(_conv_group(z, ubuf_ref, ushift_ref, wdw_ref, bdw_ref, gln_ref, bln_ref).astype(BF16),
                      w_out_ref[:D_CONV, :])

        log_a = _log_sigmoid(_dot(glr.astype(BF16), wgu_ref[...]) + bg_ref[...]) * (1.0 / GATE_TAU)
        q0, k0, v0, r0 = 2 * D_CONV, 2 * D_CONV + D_QK, 2 * D_CONV + 2 * D_QK, 2 * D_CONV + 2 * D_QK + D_GLA
        for t0 in range(0, tile, GLA_TILE):
            rows = slice(t0, t0 + GLA_TILE)
            o = _gla_subtile(z[rows, q0:k0], z[rows, k0:v0], z[rows, v0:r0], log_a[rows, :], s_ref, gnorm_ref)
            mix_ref[rows, :] = (o * _silu(z[rows, r0:])).astype(BF16)

        y = y_conv + _dot(mix_ref[...], w_out_ref[D_CONV:, :])
        o_ref[sub, :] = x + mod_ref[5:6, :] * y


def _mixer_call(x, mod, g, w_in16, wgu, bg, wdw, bdw, gln, bln, gnorm, w_out16, layer):
    bsz, seq, d = x.shape
    resident = pl.Buffered(1)

    def whole(a):
        return pl.BlockSpec(a.shape, lambda b, t: (0,) * a.ndim, pipeline_mode=resident)

    def of_layer(a):
        return pl.BlockSpec((None,) + a.shape[1:], lambda b, t: (layer, 0, 0), pipeline_mode=resident)

    return pl.pallas_call(
        _mixer_kernel,
        grid=(bsz, seq // MIX_TILE),
        in_specs=[
            pl.BlockSpec((None, MIX_TILE, d), lambda b, t: (b, t, 0)),
            pl.BlockSpec((None, N_MOD, d), lambda b, t: (b, 0, 0)),
            whole(g), of_layer(w_in16), whole(wgu), whole(bg), whole(wdw), whole(bdw),
            whole(gln), whole(bln), whole(gnorm), of_layer(w_out16),
        ],
        out_specs=pl.BlockSpec((None, MIX_TILE, d), lambda b, t: (b, t, 0)),
        out_shape=jax.ShapeDtypeStruct((bsz, seq, d), F32),
        scratch_shapes=[
            pltpu.VMEM((D_QK, D_GLA), F32),
            pltpu.VMEM((CONV_PAD + MIX_SUB, D_CONV), F32),
            pltpu.VMEM((SUBLANES - 1, MIX_SUB + CONV_PAD - SUBLANES, D_CONV), F32),
            pltpu.VMEM((MIX_SUB, D_GLA), BF16),
        ],
        compiler_params=pltpu.CompilerParams(
            dimension_semantics=("arbitrary", "arbitrary"), vmem_limit_bytes=VMEM_LIMIT_BYTES),
        name="mixer",
    )(x, mod, g, w_in16, wgu, bg, wdw, bdw, gln, bln, gnorm, w_out16)


def kernel(x, c, w_ada, b_ada, g_norm_ffn1, w_ffn1_in, w_ffn1_out, g_norm_mix, w_in, w_dw, b_dw, g_conv_ln, b_conv_ln, w_gate_up, b_gate, g_gla_norm, w_out, g_norm_ffn2, w_ffn2_in, w_ffn2_out, g_norm_final, w_ada_final, b_ada_final):
    bsz, seq, d = x.shape
    depth = w_ada.shape[0]
    assert seq % MIX_TILE == 0 and seq % FFN_TILE == 0 and FFN_TILE % FFN_SUB == 0 and MIX_TILE % MIX_SUB == 0
    assert MIX_SUB % GLA_TILE == 0 and MIX_SUB % CONV_ROWS == 0
    assert bsz <= SUBLANES
    tiles_per_batch = seq // FFN_TILE

    c_pad = jnp.pad(c, ((0, SUBLANES - bsz), (0, 0)))
    mod = _ada_call(c_pad, w_ada, b_ada[:, None, :], 1024)[:, :bsz].reshape(depth, bsz, N_MOD, d)
    fmod = _ada_call(c_pad, w_ada_final[None], b_ada_final[None, None, :], 1024)[0, :bsz].reshape(bsz, 2, d)

    w_in16 = w_in.astype(BF16)
    w_out16 = w_out.astype(BF16)
    wgu = w_gate_up.astype(BF16)

    x2d = x.reshape(bsz * seq, d)
    for l in range(depth):
        mod_l = jnp.concatenate([mod[l], fmod], axis=1)
        x2d = _ffn_call(x2d, mod_l, jnp.stack([g_norm_ffn1[l], g_norm_final]), w_ffn1_in,
                        w_ffn1_out, l, 0, tiles_per_batch, False)
        x3d = _mixer_call(
            x2d.reshape(bsz, seq, d), mod[l], g_norm_mix[l][None], w_in16,
            wgu[l], b_gate[l][None], w_dw[l], b_dw[l][None], g_conv_ln[l][None], b_conv_ln[l][None],
            g_gla_norm[l], w_out16, l)
        x2d = _ffn_call(x3d.reshape(bsz * seq, d), mod_l, jnp.stack([g_norm_ffn2[l], g_norm_final]),
                        w_ffn2_in, w_ffn2_out, l, 6, tiles_per_batch, l == depth - 1)
    return x2d.reshape(bsz, seq, d)
```

```python
import functools

import jax
import jax.numpy as jnp
from jax import lax
from jax.experimental import pallas as pl
from jax.experimental.pallas import tpu as pltpu

D_MODEL = 1024
D_FF = 2816
D_CONV = 512
CONV_WIDTH = 31
GLA_HEADS = 4
GLA_DK = 64
GLA_DV = 128
D_QK = GLA_HEADS * GLA_DK
D_GLA = GLA_HEADS * GLA_DV
D_MIX = D_CONV + D_GLA
GATE_RANK = 16
GATE_TAU = 16.0
CHUNK = 64
N_MOD = 9
EPS = 1e-6
D_MAIN = 2 * D_CONV + 2 * D_QK + 2 * D_GLA

LANES = 128
SUBLANES = 8
VMEM_LIMIT_BYTES = 56 * 1024 * 1024

FFN_TILE = 1024
FFN_SUB = 128
MIX_TILE = 512
MIX_SUB = 256
GLA_TILE = 2 * CHUNK
CONV_ROWS = 32
CONV_PAD = 32
W_STAGE_CHUNKS = 8

F32 = jnp.float32
BF16 = jnp.bfloat16


def _dot(a, b):
    return jnp.dot(a, b, preferred_element_type=F32)


def _dot_nt(a, b):
    return lax.dot_general(a, b, (((1,), (1,)), ((), ())), preferred_element_type=F32)


def _dot_tn(a, b):
    return lax.dot_general(a, b, (((0,), (0,)), ((), ())), preferred_element_type=F32)


def _sigmoid(x):
    return 1.0 / (1.0 + jnp.exp(-x))


def _silu(x):
    return x * _sigmoid(x)


def _log_sigmoid(x):
    return jnp.minimum(x, 0.0) - jnp.log(1.0 + jnp.exp(-jnp.abs(x)))


def _rms_mod(x, g, shift, scale):
    y = x * lax.rsqrt(jnp.mean(x * x, axis=-1, keepdims=True) + EPS)
    return (y * g) * (1.0 + scale) + shift


def _split_bf16(x, parts):
    out = []
    for _ in range(parts):
        p = x.astype(BF16)
        out.append(p)
        x = x - p.astype(F32)
    return out


def _ada_kernel(c_ref, w_ref, b_ref, o_ref):
    c = _silu(c_ref[...])
    o_ref[...] = _dot(c.astype(BF16), w_ref[...].astype(BF16)) + b_ref[...]


def _ada_call(c_pad, w, b, tn):
    n_layers, d, n = w.shape
    return pl.pallas_call(
        _ada_kernel,
        grid=(n_layers, n // tn),
        in_specs=[
            pl.BlockSpec((SUBLANES, d), lambda l, j: (0, 0)),
            pl.BlockSpec((None, d, tn), lambda l, j: (l, 0, j)),
            pl.BlockSpec((None, 1, tn), lambda l, j: (l, 0, j)),
        ],
        out_specs=pl.BlockSpec((None, SUBLANES, tn), lambda l, j: (l, 0, j)),
        out_shape=jax.ShapeDtypeStruct((n_layers, SUBLANES, n), F32),
        compiler_params=pltpu.CompilerParams(
            dimension_semantics=("arbitrary", "arbitrary"), vmem_limit_bytes=VMEM_LIMIT_BYTES),
        name="ada",
    )(c_pad, w, b)


def _load_as_bf16(src_hbm_ref, dst_ref, stage_ref, sem_ref):
    rows = stage_ref.shape[1]
    n_chunks = src_hbm_ref.shape[0] // rows
    assert n_chunks * rows == src_hbm_ref.shape[0] == dst_ref.shape[0]

    def chunk_copy(c):
        return pltpu.make_async_copy(src_hbm_ref.at[pl.ds(c * rows, rows), :], stage_ref.at[c % 2], sem_ref.at[c % 2])

    chunk_copy(0).start()
    for c in range(n_chunks):
        if c + 1 < n_chunks:
            chunk_copy(c + 1).start()
        chunk_copy(c).wait()
        dst_ref[c * rows:(c + 1) * rows, :] = stage_ref[c % 2].astype(BF16)


def _ffn_kernel(x_ref, mod_ref, fmod_ref, g_ref, g_final_ref, w_in_hbm_ref, w_out_hbm_ref, o_ref,
                w_in_ref, w_out_ref, stage_in_ref, stage_out_ref, sem_in_ref, sem_out_ref, *, layer, mod_row, final):
    @pl.when(pl.program_id(0) == 0)
    def _():
        _load_as_bf16(w_in_hbm_ref.at[layer], w_in_ref, stage_in_ref, sem_in_ref)
        _load_as_bf16(w_out_hbm_ref.at[layer], w_out_ref, stage_out_ref, sem_out_ref)

    shift = mod_ref[mod_row:mod_row + 1, :]
    scale = mod_ref[mod_row + 1:mod_row + 2, :]
    gate = mod_ref[mod_row + 2:mod_row + 3, :]
    for m0 in range(0, FFN_TILE, FFN_SUB):
        rows = slice(m0, m0 + FFN_SUB)
        x = x_ref[rows, :]
        h = _rms_mod(x, g_ref[layer:layer + 1, :], shift, scale).astype(BF16)
        gu = _dot(h, w_in_ref[...])
        act = (_silu(gu[:, :D_FF]) * gu[:, D_FF:]).astype(BF16)
        y = x + (0.5 * gate) * _dot(act, w_out_ref[...])
        if final:
            y = _rms_mod(y, g_final_ref[...], fmod_ref[0:1, :], fmod_ref[1:2, :])
        o_ref[rows, :] = y


def _ffn_call(x2d, mod, fmod, g_all, g_final, w_in_all, w_out_all, layer, mod_row, tiles_per_batch, final):
    n_tok, d = x2d.shape
    w_in_shape, w_out_shape = w_in_all.shape[1:], w_out_all.shape[1:]
    assert w_in_shape[0] % W_STAGE_CHUNKS == 0 and w_out_shape[0] % W_STAGE_CHUNKS == 0
    return pl.pallas_call(
        functools.partial(_ffn_kernel, layer=layer, mod_row=mod_row, final=final),
        grid=(n_tok // FFN_TILE,),
        in_specs=[
            pl.BlockSpec((FFN_TILE, d), lambda i: (i, 0)),
            pl.BlockSpec((None, None, N_MOD, d), lambda i: (layer, i // tiles_per_batch, 0, 0)),
            pl.BlockSpec((None, 2, d), lambda i: (i // tiles_per_batch, 0, 0)),
            pl.BlockSpec(g_all.shape, lambda i: (0, 0)),
            pl.BlockSpec((1, d), lambda i: (0, 0)),
            pl.BlockSpec(memory_space=pl.ANY),
            pl.BlockSpec(memory_space=pl.ANY),
        ],
        out_specs=pl.BlockSpec((FFN_TILE, d), lambda i: (i, 0)),
        out_shape=jax.ShapeDtypeStruct((n_tok, d), F32),
        scratch_shapes=[
            pltpu.VMEM(w_in_shape, BF16),
            pltpu.VMEM(w_out_shape, BF16),
            pltpu.VMEM((2, w_in_shape[0] // W_STAGE_CHUNKS, w_in_shape[1]), F32),
            pltpu.VMEM((2, w_out_shape[0] // W_STAGE_CHUNKS, w_out_shape[1]), F32),
            pltpu.SemaphoreType.DMA((2,)),
            pltpu.SemaphoreType.DMA((2,)),
        ],
        compiler_params=pltpu.CompilerParams(
            dimension_semantics=("arbitrary",), vmem_limit_bytes=VMEM_LIMIT_BYTES),
        name="ffn",
    )(x2d, mod, fmod, g_all, g_final, w_in_all, w_out_all)


def _conv_group(z, ubuf_ref, ushift_ref, wdw_ref, b_dw, g_ln, b_ln):
    tile = z.shape[0]
    ubuf_ref[CONV_PAD:CONV_PAD + tile, :] = z[:, :D_CONV] * _sigmoid(z[:, D_CONV:2 * D_CONV])
    first = CONV_PAD - (CONV_WIDTH - 1)
    shifted_rows = ushift_ref.shape[1]
    for s in range(1, SUBLANES):
        ushift_ref[s - 1] = ubuf_ref[s:s + shifted_rows, :]
    blocks = []
    for r0 in range(0, tile, CONV_ROWS):
        acc = jnp.zeros((CONV_ROWS, D_CONV), F32)
        for k in range(CONV_WIDTH):
            base, s = (first + k) // SUBLANES * SUBLANES, (first + k) % SUBLANES
            rows = slice(r0 + base, r0 + base + CONV_ROWS)
            tap = ubuf_ref[rows, :] if s == 0 else ushift_ref[s - 1, rows, :]
            acc = acc + wdw_ref[k:k + 1, :] * tap
        blocks.append(acc)
    y = jnp.concatenate(blocks, axis=0) + b_dw
    ubuf_ref[0:CONV_PAD, :] = ubuf_ref[tile:tile + CONV_PAD, :]
    yc = y - jnp.mean(y, axis=-1, keepdims=True)
    yn = yc * lax.rsqrt(jnp.mean(yc * yc, axis=-1, keepdims=True) + EPS)
    return _silu(yn * g_ln + b_ln)


def _gla_subtile(q, k, v, log_a, s_ref, gnorm_ref):
    t = GLA_TILE
    row = lax.broadcasted_iota(jnp.int32, (t, t), 0)
    col = lax.broadcasted_iota(jnp.int32, (t, t), 1)
    lower = col <= row
    tri = jnp.where(lower, 1.0, 0.0).astype(BF16)
    ones = jnp.ones((t, LANES), BF16)
    parts = _split_bf16(log_a, 2)
    cum = sum(_dot(tri, p) for p in parts)
    total_cols = sum(_dot_tn(p, ones) for p in parts)
    mid = cum[CHUNK - 1:CHUNK, :]
    end = cum[t - 1:t, :]
    rel = cum - mid
    e_pos = jnp.exp(rel)
    e_neg = jnp.exp(-rel)
    qs = q * (GLA_DK ** -0.5)
    q_fwd = (qs * e_pos).astype(BF16)
    q_bwd = (qs * e_neg).astype(BF16)
    q_in = (qs * jnp.exp(cum)).astype(BF16)
    k_fwd = (k * e_pos).astype(BF16)
    k_bwd = (k * e_neg).astype(BF16)
    k_out = (k * jnp.exp(end - cum)).astype(BF16)
    v16 = v.astype(BF16)

    same_chunk = (row >= CHUNK) == (col >= CHUNK)
    lane_head = lax.broadcasted_iota(jnp.int32, (t, D_QK), 1) // GLA_DK
    zero16 = jnp.zeros((), BF16)

    def per_head_keys(k16):
        return jnp.concatenate([jnp.where(lane_head == h, k16, zero16) for h in range(GLA_HEADS)], axis=0)

    att_fwd = _dot_nt(q_fwd, per_head_keys(k_bwd))
    att_bwd = _dot_nt(q_bwd, per_head_keys(k_fwd))
    s_old = s_ref[...]
    o_state = _dot(q_in, s_old.astype(BF16))
    outs = []
    for h in range(GLA_HEADS):
        cols = slice(h * t, (h + 1) * t)
        vh = v16[:, h * GLA_DV:(h + 1) * GLA_DV]
        att = jnp.where(lower, att_fwd[:, cols], jnp.where(same_chunk, att_bwd[:, cols], 0.0))
        o = _dot(att.astype(BF16), vh) + o_state[:, h * GLA_DV:(h + 1) * GLA_DV]
        o = o * lax.rsqrt(jnp.mean(o * o, axis=-1, keepdims=True) + EPS) * gnorm_ref[h:h + 1, :]
        outs.append(o)
    upd = _dot_tn(k_out, v16)
    decay = jnp.exp(total_cols)
    for h in range(GLA_HEADS):
        rows, cols = slice(h * GLA_DK, (h + 1) * GLA_DK), slice(h * GLA_DV, (h + 1) * GLA_DV)
        s_ref[rows, cols] = decay[rows, :] * s_old[rows, cols] + upd[rows, cols]
    return jnp.concatenate(outs, axis=-1)


def _mixer_kernel(x_ref, mod_ref, g_ref, w_in_ref, wgu_ref, bg_ref, wdw_ref, bdw_ref,
                  gln_ref, bln_ref, gnorm_ref, w_out_ref, o_ref, s_ref, ubuf_ref, ushift_ref, mix_ref, *, layer):
    this = slice(layer, layer + 1)
    @pl.when(pl.program_id(1) == 0)
    def _():
        s_ref[...] = jnp.zeros_like(s_ref)
        ubuf_ref[0:CONV_PAD, :] = jnp.zeros((CONV_PAD, D_CONV), F32)

    for m0 in range(0, MIX_TILE, MIX_SUB):
        sub = slice(m0, m0 + MIX_SUB)
        x = x_ref[sub, :]
        tile = MIX_SUB
        h = _rms_mod(x, g_ref[this, :], mod_ref[3:4, :], mod_ref[4:5, :]).astype(BF16)
        z = _dot(h, w_in_ref[:, :D_MAIN])
        glr = _dot(h, w_in_ref[:, D_MAIN:])
        y_conv = _dot(_conv_group(z, ubuf_ref, ushift_ref, wdw_ref, bdw_ref[this, :], gln_ref[this, :],
                                  bln_ref[this, :]).astype(BF16),
                      w_out_ref[:D_CONV, :])

        log_a = _log_sigmoid(_dot(glr.astype(BF16), wgu_ref[...]) + bg_ref[this, :]) * (1.0 / GATE_TAU)
        q0, k0, v0, r0 = 2 * D_CONV, 2 * D_CONV + D_QK, 2 * D_CONV + 2 * D_QK, 2 * D_CONV + 2 * D_QK + D_GLA
        for t0 in range(0, tile, GLA_TILE):
            rows = slice(t0, t0 + GLA_TILE)
            o = _gla_subtile(z[rows, q0:k0], z[rows, k0:v0], z[rows, v0:r0], log_a[rows, :], s_ref, gnorm_ref)
            mix_ref[rows, :] = (o * _silu(z[rows, r0:])).astype(BF16)

        y = y_conv + _dot(mix_ref[...], w_out_ref[D_CONV:, :])
        o_ref[sub, :] = x + mod_ref[5:6, :] * y


def _mixer_call(x, mod, g, w_in16, wgu, bg, wdw, bdw, gln, bln, gnorm, w_out16, layer):
    bsz, seq, d = x.shape
    resident = pl.Buffered(1)

    def whole(a):
        return pl.BlockSpec(a.shape, lambda b, t: (0,) * a.ndim, pipeline_mode=resident)

    def of_layer(a):
        return pl.BlockSpec((None,) + a.shape[1:], lambda b, t: (layer, 0, 0), pipeline_mode=resident)

    return pl.pallas_call(
        functools.partial(_mixer_kernel, layer=layer),
        grid=(bsz, seq // MIX_TILE),
        in_specs=[
            pl.BlockSpec((None, MIX_TILE, d), lambda b, t: (b, t, 0)),
            pl.BlockSpec((None, None, N_MOD, d), lambda b, t: (layer, b, 0, 0)),
            whole(g), of_layer(w_in16), of_layer(wgu), whole(bg), of_layer(wdw), whole(bdw),
            whole(gln), whole(bln), of_layer(gnorm), of_layer(w_out16),
        ],
        out_specs=pl.BlockSpec((None, MIX_TILE, d), lambda b, t: (b, t, 0)),
        out_shape=jax.ShapeDtypeStruct((bsz, seq, d), F32),
        scratch_shapes=[
            pltpu.VMEM((D_QK, D_GLA), F32),
            pltpu.VMEM((CONV_PAD + MIX_SUB, D_CONV), F32),
            pltpu.VMEM((SUBLANES - 1, MIX_SUB + CONV_PAD - SUBLANES, D_CONV), F32),
            pltpu.VMEM((MIX_SUB, D_GLA), BF16),
        ],
        compiler_params=pltpu.CompilerParams(
            dimension_semantics=("arbitrary", "arbitrary"), vmem_limit_bytes=VMEM_LIMIT_BYTES),
        name="mixer",
    )(x, mod, g, w_in16, wgu, bg, wdw, bdw, gln, bln, gnorm, w_out16)


def kernel(x, c, w_ada, b_ada, g_norm_ffn1, w_ffn1_in, w_ffn1_out, g_norm_mix, w_in, w_dw, b_dw, g_conv_ln, b_conv_ln, w_gate_up, b_gate, g_gla_norm, w_out, g_norm_ffn2, w_ffn2_in, w_ffn2_out, g_norm_final, w_ada_final, b_ada_final):
    bsz, seq, d = x.shape
    depth = w_ada.shape[0]
    assert seq % MIX_TILE == 0 and seq % FFN_TILE == 0 and FFN_TILE % FFN_SUB == 0 and MIX_TILE % MIX_SUB == 0
    assert MIX_SUB % GLA_TILE == 0 and MIX_SUB % CONV_ROWS == 0
    assert bsz <= SUBLANES
    tiles_per_batch = seq // FFN_TILE

    c_pad = jnp.pad(c, ((0, SUBLANES - bsz), (0, 0)))
    mod = _ada_call(c_pad, w_ada, b_ada[:, None, :], 1024)[:, :bsz].reshape(depth, bsz, N_MOD, d)
    fmod = _ada_call(c_pad, w_ada_final[None], b_ada_final[None, None, :], 1024)[0, :bsz].reshape(bsz, 2, d)

    w_in16 = w_in.astype(BF16)
    w_out16 = w_out.astype(BF16)
    wgu = w_gate_up.astype(BF16)

    x2d = x.reshape(bsz * seq, d)
    g_final = g_norm_final[None]
    for l in range(depth):
        x2d = _ffn_call(x2d, mod, fmod, g_norm_ffn1, g_final, w_ffn1_in, w_ffn1_out, l, 0, tiles_per_batch, False)
        x3d = _mixer_call(x2d.reshape(bsz, seq, d), mod, g_norm_mix, w_in16, wgu, b_gate, w_dw, b_dw, g_conv_ln,
                          b_conv_ln, g_gla_norm, w_out16, l)
        x2d = _ffn_call(x3d.reshape(bsz * seq, d), mod, fmod, g_norm_ffn2, g_final, w_ffn2_in, w_ffn2_out, l, 6,
                        tiles_per_batch, l == depth - 1)
    return x2d.reshape(bsz, seq, d)
```

```python
import functools

import jax
import jax.numpy as jnp
from jax import lax
from jax.experimental import pallas as pl
from jax.experimental.pallas import tpu as pltpu

D_MODEL = 1024
D_FF = 2816
D_CONV = 512
CONV_WIDTH = 31
GLA_HEADS = 4
GLA_DK = 64
GLA_DV = 128
D_QK = GLA_HEADS * GLA_DK
D_GLA = GLA_HEADS * GLA_DV
D_MIX = D_CONV + D_GLA
GATE_RANK = 16
GATE_TAU = 16.0
CHUNK = 64
N_MOD = 9
EPS = 1e-6
D_MAIN = 2 * D_CONV + 2 * D_QK + 2 * D_GLA

LANES = 128
SUBLANES = 8
VMEM_LIMIT_BYTES = 56 * 1024 * 1024

FFN_TILE = 1024
FFN_SUB = 128
MIX_TILE = 512
MIX_SUB = 256
GLA_TILE = 2 * CHUNK
CONV_ROWS = 32
CONV_PAD = 32
W_STAGE_CHUNKS = 8

F32 = jnp.float32
BF16 = jnp.bfloat16


def _dot(a, b):
    return jnp.dot(a, b, preferred_element_type=F32)


def _dot_nt(a, b):
    return lax.dot_general(a, b, (((1,), (1,)), ((), ())), preferred_element_type=F32)


def _dot_tn(a, b):
    return lax.dot_general(a, b, (((0,), (0,)), ((), ())), preferred_element_type=F32)


def _sigmoid(x):
    return 1.0 / (1.0 + jnp.exp(-x))


def _silu(x):
    return x * _sigmoid(x)


def _log_sigmoid(x):
    return jnp.minimum(x, 0.0) - jnp.log(1.0 + jnp.exp(-jnp.abs(x)))


def _rms_mod(x, g, shift, scale):
    y = x * lax.rsqrt(jnp.mean(x * x, axis=-1, keepdims=True) + EPS)
    return (y * g) * (1.0 + scale) + shift


def _split_bf16(x, parts):
    out = []
    for _ in range(parts):
        p = x.astype(BF16)
        out.append(p)
        x = x - p.astype(F32)
    return out


def _ada_kernel(c_ref, w_ref, b_ref, o_ref):
    c = _silu(c_ref[...])
    o_ref[...] = _dot(c.astype(BF16), w_ref[...].astype(BF16)) + b_ref[...]


def _ada_call(c_pad, w, b, tn):
    n_layers, d, n = w.shape
    return pl.pallas_call(
        _ada_kernel,
        grid=(n_layers, n // tn),
        in_specs=[
            pl.BlockSpec((SUBLANES, d), lambda l, j: (0, 0)),
            pl.BlockSpec((None, d, tn), lambda l, j: (l, 0, j)),
            pl.BlockSpec((None, 1, tn), lambda l, j: (l, 0, j)),
        ],
        out_specs=pl.BlockSpec((None, SUBLANES, tn), lambda l, j: (l, 0, j)),
        out_shape=jax.ShapeDtypeStruct((n_layers, SUBLANES, n), F32),
        compiler_params=pltpu.CompilerParams(
            dimension_semantics=("arbitrary", "arbitrary"), vmem_limit_bytes=VMEM_LIMIT_BYTES),
        name="ada",
    )(c_pad, w, b)


def _load_as_bf16(src_hbm_ref, dst_ref, stage_ref, sem_ref):
    rows = stage_ref.shape[1]
    n_chunks = src_hbm_ref.shape[0] // rows
    assert n_chunks * rows == src_hbm_ref.shape[0] == dst_ref.shape[0]

    def chunk_copy(c):
        return pltpu.make_async_copy(src_hbm_ref.at[pl.ds(c * rows, rows), :], stage_ref.at[c % 2], sem_ref.at[c % 2])

    chunk_copy(0).start()
    for c in range(n_chunks):
        if c + 1 < n_chunks:
            chunk_copy(c + 1).start()
        chunk_copy(c).wait()
        dst_ref[c * rows:(c + 1) * rows, :] = stage_ref[c % 2].astype(BF16)


def _ffn_kernel(x_ref, mod_ref, fmod_ref, g_ref, g_final_ref, w_in_hbm_ref, w_out_hbm_ref, o_ref,
                w_in_ref, w_out_ref, stage_in_ref, stage_out_ref, sem_in_ref, sem_out_ref, *, layer, mod_row, final):
    @pl.when(pl.program_id(0) == 0)
    def _():
        _load_as_bf16(w_in_hbm_ref.at[layer], w_in_ref, stage_in_ref, sem_in_ref)
        _load_as_bf16(w_out_hbm_ref.at[layer], w_out_ref, stage_out_ref, sem_out_ref)

    shift = mod_ref[mod_row:mod_row + 1, :]
    scale = mod_ref[mod_row + 1:mod_row + 2, :]
    gate = mod_ref[mod_row + 2:mod_row + 3, :]
    for m0 in range(0, FFN_TILE, FFN_SUB):
        rows = slice(m0, m0 + FFN_SUB)
        x = x_ref[rows, :]
        h = _rms_mod(x, g_ref[layer:layer + 1, :], shift, scale).astype(BF16)
        gu = _dot(h, w_in_ref[...])
        act = (_silu(gu[:, :D_FF]) * gu[:, D_FF:]).astype(BF16)
        y = x + (0.5 * gate) * _dot(act, w_out_ref[...])
        if final:
            y = _rms_mod(y, g_final_ref[...], fmod_ref[0:1, :], fmod_ref[1:2, :])
        o_ref[rows, :] = y


def _ffn_call(x2d, mod, fmod, g_all, g_final, w_in_all, w_out_all, layer, mod_row, tiles_per_batch, final, in_place):
    n_tok, d = x2d.shape
    w_in_shape, w_out_shape = w_in_all.shape[1:], w_out_all.shape[1:]
    assert w_in_shape[0] % W_STAGE_CHUNKS == 0 and w_out_shape[0] % W_STAGE_CHUNKS == 0
    return pl.pallas_call(
        functools.partial(_ffn_kernel, layer=layer, mod_row=mod_row, final=final),
        grid=(n_tok // FFN_TILE,),
        in_specs=[
            pl.BlockSpec((FFN_TILE, d), lambda i: (i, 0)),
            pl.BlockSpec((None, None, N_MOD, d), lambda i: (layer, i // tiles_per_batch, 0, 0)),
            pl.BlockSpec((None, 2, d), lambda i: (i // tiles_per_batch, 0, 0)),
            pl.BlockSpec(g_all.shape, lambda i: (0, 0)),
            pl.BlockSpec((1, d), lambda i: (0, 0)),
            pl.BlockSpec(memory_space=pl.ANY),
            pl.BlockSpec(memory_space=pl.ANY),
        ],
        out_specs=pl.BlockSpec((FFN_TILE, d), lambda i: (i, 0)),
        out_shape=jax.ShapeDtypeStruct((n_tok, d), F32),
        scratch_shapes=[
            pltpu.VMEM(w_in_shape, BF16),
            pltpu.VMEM(w_out_shape, BF16),
            pltpu.VMEM((2, w_in_shape[0] // W_STAGE_CHUNKS, w_in_shape[1]), F32),
            pltpu.VMEM((2, w_out_shape[0] // W_STAGE_CHUNKS, w_out_shape[1]), F32),
            pltpu.SemaphoreType.DMA((2,)),
            pltpu.SemaphoreType.DMA((2,)),
        ],
        input_output_aliases={0: 0} if in_place else {},
        compiler_params=pltpu.CompilerParams(
            dimension_semantics=("arbitrary",), vmem_limit_bytes=VMEM_LIMIT_BYTES),
        name="ffn",
    )(x2d, mod, fmod, g_all, g_final, w_in_all, w_out_all)


def _conv_group(z, ubuf_ref, ushift_ref, wdw_ref, b_dw, g_ln, b_ln):
    tile = z.shape[0]
    ubuf_ref[CONV_PAD:CONV_PAD + tile, :] = z[:, :D_CONV] * _sigmoid(z[:, D_CONV:2 * D_CONV])
    first = CONV_PAD - (CONV_WIDTH - 1)
    shifted_rows = ushift_ref.shape[1]
    for s in range(1, SUBLANES):
        ushift_ref[s - 1] = ubuf_ref[s:s + shifted_rows, :]
    blocks = []
    for r0 in range(0, tile, CONV_ROWS):
        acc = jnp.zeros((CONV_ROWS, D_CONV), F32)
        for k in range(CONV_WIDTH):
            base, s = (first + k) // SUBLANES * SUBLANES, (first + k) % SUBLANES
            rows = slice(r0 + base, r0 + base + CONV_ROWS)
            tap = ubuf_ref[rows, :] if s == 0 else ushift_ref[s - 1, rows, :]
            acc = acc + wdw_ref[k:k + 1, :] * tap
        blocks.append(acc)
    y = jnp.concatenate(blocks, axis=0) + b_dw
    ubuf_ref[0:CONV_PAD, :] = ubuf_ref[tile:tile + CONV_PAD, :]
    yc = y - jnp.mean(y, axis=-1, keepdims=True)
    yn = yc * lax.rsqrt(jnp.mean(yc * yc, axis=-1, keepdims=True) + EPS)
    return _silu(yn * g_ln + b_ln)


def _gla_subtile(q, k, v, log_a, s_ref, gnorm_ref):
    t = GLA_TILE
    row = lax.broadcasted_iota(jnp.int32, (t, t), 0)
    col = lax.broadcasted_iota(jnp.int32, (t, t), 1)
    lower = col <= row
    tri = jnp.where(lower, 1.0, 0.0).astype(BF16)
    ones = jnp.ones((t, LANES), BF16)
    parts = _split_bf16(log_a, 2)
    cum = sum(_dot(tri, p) for p in parts)
    total_cols = sum(_dot_tn(p, ones) for p in parts)
    mid = cum[CHUNK - 1:CHUNK, :]
    end = cum[t - 1:t, :]
    rel = cum - mid
    e_pos = jnp.exp(rel)
    e_neg = jnp.exp(-rel)
    qs = q * (GLA_DK ** -0.5)
    q_fwd = (qs * e_pos).astype(BF16)
    q_bwd = (qs * e_neg).astype(BF16)
    q_in = (qs * jnp.exp(cum)).astype(BF16)
    k_fwd = (k * e_pos).astype(BF16)
    k_bwd = (k * e_neg).astype(BF16)
    k_out = (k * jnp.exp(end - cum)).astype(BF16)
    v16 = v.astype(BF16)

    same_chunk = (row >= CHUNK) == (col >= CHUNK)
    lane_head = lax.broadcasted_iota(jnp.int32, (t, D_QK), 1) // GLA_DK
    zero16 = jnp.zeros((), BF16)

    def per_head_keys(k16):
        return jnp.concatenate([jnp.where(lane_head == h, k16, zero16) for h in range(GLA_HEADS)], axis=0)

    att_fwd = _dot_nt(q_fwd, per_head_keys(k_bwd))
    att_bwd = _dot_nt(q_bwd, per_head_keys(k_fwd))
    s_old = s_ref[...]
    o_state = _dot(q_in, s_old.astype(BF16))
    outs = []
    for h in range(GLA_HEADS):
        cols = slice(h * t, (h + 1) * t)
        vh = v16[:, h * GLA_DV:(h + 1) * GLA_DV]
        att = jnp.where(lower, att_fwd[:, cols], jnp.where(same_chunk, att_bwd[:, cols], 0.0))
        o = _dot(att.astype(BF16), vh) + o_state[:, h * GLA_DV:(h + 1) * GLA_DV]
        o = o * lax.rsqrt(jnp.mean(o * o, axis=-1, keepdims=True) + EPS) * gnorm_ref[h:h + 1, :]
        outs.append(o)
    upd = _dot_tn(k_out, v16)
    decay = jnp.exp(total_cols)
    for h in range(GLA_HEADS):
        rows, cols = slice(h * GLA_DK, (h + 1) * GLA_DK), slice(h * GLA_DV, (h + 1) * GLA_DV)
        s_ref[rows, cols] = decay[rows, :] * s_old[rows, cols] + upd[rows, cols]
    return jnp.concatenate(outs, axis=-1)


def _mixer_kernel(x_ref, mod_ref, g_ref, w_in_ref, wgu_ref, bg_ref, wdw_ref, bdw_ref,
                  gln_ref, bln_ref, gnorm_ref, w_out_ref, o_ref, s_ref, ubuf_ref, ushift_ref, mix_ref, *, layer):
    this = slice(layer, layer + 1)
    @pl.when(pl.program_id(1) == 0)
    def _():
        s_ref[...] = jnp.zeros_like(s_ref)
        ubuf_ref[0:CONV_PAD, :] = jnp.zeros((CONV_PAD, D_CONV), F32)

    for m0 in range(0, MIX_TILE, MIX_SUB):
        sub = slice(m0, m0 + MIX_SUB)
        x = x_ref[sub, :]
        tile = MIX_SUB
        h = _rms_mod(x, g_ref[this, :], mod_ref[3:4, :], mod_ref[4:5, :]).astype(BF16)
        z = _dot(h, w_in_ref[:, :D_MAIN])
        glr = _dot(h, w_in_ref[:, D_MAIN:])
        y_conv = _dot(_conv_group(z, ubuf_ref, ushift_ref, wdw_ref, bdw_ref[this, :], gln_ref[this, :],
                                  bln_ref[this, :]).astype(BF16),
                      w_out_ref[:D_CONV, :])

        log_a = _log_sigmoid(_dot(glr.astype(BF16), wgu_ref[...]) + bg_ref[this, :]) * (1.0 / GATE_TAU)
        q0, k0, v0, r0 = 2 * D_CONV, 2 * D_CONV + D_QK, 2 * D_CONV + 2 * D_QK, 2 * D_CONV + 2 * D_QK + D_GLA
        for t0 in range(0, tile, GLA_TILE):
            rows = slice(t0, t0 + GLA_TILE)
            o = _gla_subtile(z[rows, q0:k0], z[rows, k0:v0], z[rows, v0:r0], log_a[rows, :], s_ref, gnorm_ref)
            mix_ref[rows, :] = (o * _silu(z[rows, r0:])).astype(BF16)

        y = y_conv + _dot(mix_ref[...], w_out_ref[D_CONV:, :])
        o_ref[sub, :] = x + mod_ref[5:6, :] * y


def _mixer_call(x, mod, g, w_in16, wgu, bg, wdw, bdw, gln, bln, gnorm, w_out16, layer):
    bsz, seq, d = x.shape
    resident = pl.Buffered(1)

    def whole(a):
        return pl.BlockSpec(a.shape, lambda b, t: (0,) * a.ndim, pipeline_mode=resident)

    def of_layer(a):
        return pl.BlockSpec((None,) + a.shape[1:], lambda b, t: (layer, 0, 0), pipeline_mode=resident)

    return pl.pallas_call(
        functools.partial(_mixer_kernel, layer=layer),
        grid=(bsz, seq // MIX_TILE),
        in_specs=[
            pl.BlockSpec((None, MIX_TILE, d), lambda b, t: (b, t, 0)),
            pl.BlockSpec((None, None, N_MOD, d), lambda b, t: (layer, b, 0, 0)),
            whole(g), of_layer(w_in16), of_layer(wgu), whole(bg), of_layer(wdw), whole(bdw),
            whole(gln), whole(bln), of_layer(gnorm), of_layer(w_out16),
        ],
        out_specs=pl.BlockSpec((None, MIX_TILE, d), lambda b, t: (b, t, 0)),
        out_shape=jax.ShapeDtypeStruct((bsz, seq, d), F32),
        scratch_shapes=[
            pltpu.VMEM((D_QK, D_GLA), F32),
            pltpu.VMEM((CONV_PAD + MIX_SUB, D_CONV), F32),
            pltpu.VMEM((SUBLANES - 1, MIX_SUB + CONV_PAD - SUBLANES, D_CONV), F32),
            pltpu.VMEM((MIX_SUB, D_GLA), BF16),
        ],
        input_output_aliases={0: 0},
        compiler_params=pltpu.CompilerParams(
            dimension_semantics=("arbitrary", "arbitrary"), vmem_limit_bytes=VMEM_LIMIT_BYTES),
        name="mixer",
    )(x, mod, g, w_in16, wgu, bg, wdw, bdw, gln, bln, gnorm, w_out16)


def kernel(x, c, w_ada, b_ada, g_norm_ffn1, w_ffn1_in, w_ffn1_out, g_norm_mix, w_in, w_dw, b_dw, g_conv_ln, b_conv_ln, w_gate_up, b_gate, g_gla_norm, w_out, g_norm_ffn2, w_ffn2_in, w_ffn2_out, g_norm_final, w_ada_final, b_ada_final):
    bsz, seq, d = x.shape
    depth = w_ada.shape[0]
    assert seq % MIX_TILE == 0 and seq % FFN_TILE == 0 and FFN_TILE % FFN_SUB == 0 and MIX_TILE % MIX_SUB == 0
    assert MIX_SUB % GLA_TILE == 0 and MIX_SUB % CONV_ROWS == 0
    assert bsz <= SUBLANES
    tiles_per_batch = seq // FFN_TILE

    c_pad = jnp.pad(c, ((0, SUBLANES - bsz), (0, 0)))
    mod = _ada_call(c_pad, w_ada, b_ada[:, None, :], 1024)[:, :bsz].reshape(depth, bsz, N_MOD, d)
    fmod = _ada_call(c_pad, w_ada_final[None], b_ada_final[None, None, :], 1024)[0, :bsz].reshape(bsz, 2, d)

    w_in16 = w_in.astype(BF16)
    w_out16 = w_out.astype(BF16)
    wgu = w_gate_up.astype(BF16)

    x2d = x.reshape(bsz * seq, d)
    g_final = g_norm_final[None]
    for l in range(depth):
        x2d = _ffn_call(x2d, mod, fmod, g_norm_ffn1, g_final, w_ffn1_in, w_ffn1_out, l, 0, tiles_per_batch, False, l > 0)
        x3d = _mixer_call(x2d.reshape(bsz, seq, d), mod, g_norm_mix, w_in16, wgu, b_gate, w_dw, b_dw, g_conv_ln,
                          b_conv_ln, g_gla_norm, w_out16, l)
        x2d = _ffn_call(x3d.reshape(bsz * seq, d), mod, fmod, g_norm_ffn2, g_final, w_ffn2_in, w_ffn2_out, l, 6,
                        tiles_per_batch, l == depth - 1, True)
    return x2d.reshape(bsz, seq, d)
```

```python
import functools

import jax
import jax.numpy as jnp
from jax import lax
from jax.experimental import pallas as pl
from jax.experimental.pallas import tpu as pltpu

D_MODEL = 1024
D_FF = 2816
D_CONV = 512
CONV_WIDTH = 31
GLA_HEADS = 4
GLA_DK = 64
GLA_DV = 128
D_QK = GLA_HEADS * GLA_DK
D_GLA = GLA_HEADS * GLA_DV
D_MIX = D_CONV + D_GLA
GATE_RANK = 16
GATE_TAU = 16.0
CHUNK = 64
N_MOD = 9
EPS = 1e-6
D_MAIN = 2 * D_CONV + 2 * D_QK + 2 * D_GLA

LANES = 128
SUBLANES = 8
VMEM_LIMIT_BYTES = 56 * 1024 * 1024

FFN_TILE = 1024
FFN_SUB = 128
MIX_TILE = 512
MIX_SUB = 256
GLA_TILE = 2 * CHUNK
CONV_ROWS = 32
CONV_PAD = 32
ADA_COLS = 2304
W_STAGE_CHUNKS = 8

F32 = jnp.float32
BF16 = jnp.bfloat16


def _dot(a, b):
    return jnp.dot(a, b, preferred_element_type=F32)


def _dot_nt(a, b):
    return lax.dot_general(a, b, (((1,), (1,)), ((), ())), preferred_element_type=F32)


def _dot_tn(a, b):
    return lax.dot_general(a, b, (((0,), (0,)), ((), ())), preferred_element_type=F32)


def _sigmoid(x):
    return 1.0 / (1.0 + jnp.exp(-x))


def _silu(x):
    return x * _sigmoid(x)


def _log_sigmoid(x):
    return jnp.minimum(x, 0.0) - jnp.log(1.0 + jnp.exp(-jnp.abs(x)))


def _rms_mod(x, g, shift, scale):
    y = x * lax.rsqrt(jnp.mean(x * x, axis=-1, keepdims=True) + EPS)
    return (y * g) * (1.0 + scale) + shift


def _split_bf16(x, parts):
    out = []
    for _ in range(parts):
        p = x.astype(BF16)
        out.append(p)
        x = x - p.astype(F32)
    return out


def _ada_kernel(c_ref, w_ref, b_ref, o_ref):
    c = _silu(c_ref[...])
    o_ref[...] = _dot(c.astype(BF16), w_ref[...].astype(BF16)) + b_ref[...]


def _ada_call(c_pad, w, b, tn):
    n_layers, d, n = w.shape
    return pl.pallas_call(
        _ada_kernel,
        grid=(n_layers, n // tn),
        in_specs=[
            pl.BlockSpec((SUBLANES, d), lambda l, j: (0, 0)),
            pl.BlockSpec((None, d, tn), lambda l, j: (l, 0, j)),
            pl.BlockSpec((None, 1, tn), lambda l, j: (l, 0, j)),
        ],
        out_specs=pl.BlockSpec((None, SUBLANES, tn), lambda l, j: (l, 0, j)),
        out_shape=jax.ShapeDtypeStruct((n_layers, SUBLANES, n), F32),
        compiler_params=pltpu.CompilerParams(
            dimension_semantics=("arbitrary", "arbitrary"), vmem_limit_bytes=VMEM_LIMIT_BYTES),
        name="ada",
    )(c_pad, w, b)


def _load_as_bf16(src_hbm_ref, dst_ref, stage_ref, sem_ref):
    rows = stage_ref.shape[1]
    n_chunks = src_hbm_ref.shape[0] // rows
    assert n_chunks * rows == src_hbm_ref.shape[0] == dst_ref.shape[0]

    def chunk_copy(c):
        return pltpu.make_async_copy(src_hbm_ref.at[pl.ds(c * rows, rows), :], stage_ref.at[c % 2], sem_ref.at[c % 2])

    chunk_copy(0).start()
    for c in range(n_chunks):
        if c + 1 < n_chunks:
            chunk_copy(c + 1).start()
        chunk_copy(c).wait()
        dst_ref[c * rows:(c + 1) * rows, :] = stage_ref[c % 2].astype(BF16)


def _ffn_kernel(x_ref, mod_ref, fmod_ref, g_ref, g_final_ref, w_in_hbm_ref, w_out_hbm_ref, o_ref,
                w_in_ref, w_out_ref, stage_in_ref, stage_out_ref, sem_in_ref, sem_out_ref, *, layer, mod_row, final):
    @pl.when(pl.program_id(0) == 0)
    def _():
        _load_as_bf16(w_in_hbm_ref.at[layer], w_in_ref, stage_in_ref, sem_in_ref)
        _load_as_bf16(w_out_hbm_ref.at[layer], w_out_ref, stage_out_ref, sem_out_ref)

    shift = mod_ref[mod_row:mod_row + 1, :]
    scale = mod_ref[mod_row + 1:mod_row + 2, :]
    gate = mod_ref[mod_row + 2:mod_row + 3, :]
    for m0 in range(0, FFN_TILE, FFN_SUB):
        rows = slice(m0, m0 + FFN_SUB)
        x = x_ref[rows, :]
        h = _rms_mod(x, g_ref[layer:layer + 1, :], shift, scale).astype(BF16)
        gu = _dot(h, w_in_ref[...])
        act = (_silu(gu[:, :D_FF]) * gu[:, D_FF:]).astype(BF16)
        y = x + (0.5 * gate) * _dot(act, w_out_ref[...])
        if final:
            y = _rms_mod(y, g_final_ref[...], fmod_ref[0:1, :], fmod_ref[1:2, :])
        o_ref[rows, :] = y


def _ffn_call(x2d, mod, fmod, g_all, g_final, w_in_all, w_out_all, layer, mod_row, tiles_per_batch, final, in_place):
    n_tok, d = x2d.shape
    w_in_shape, w_out_shape = w_in_all.shape[1:], w_out_all.shape[1:]
    assert w_in_shape[0] % W_STAGE_CHUNKS == 0 and w_out_shape[0] % W_STAGE_CHUNKS == 0
    return pl.pallas_call(
        functools.partial(_ffn_kernel, layer=layer, mod_row=mod_row, final=final),
        grid=(n_tok // FFN_TILE,),
        in_specs=[
            pl.BlockSpec((FFN_TILE, d), lambda i: (i, 0)),
            pl.BlockSpec((None, None, N_MOD, d), lambda i: (layer, i // tiles_per_batch, 0, 0)),
            pl.BlockSpec((None, 2, d), lambda i: (i // tiles_per_batch, 0, 0)),
            pl.BlockSpec(g_all.shape, lambda i: (0, 0)),
            pl.BlockSpec((1, d), lambda i: (0, 0)),
            pl.BlockSpec(memory_space=pl.ANY),
            pl.BlockSpec(memory_space=pl.ANY),
        ],
        out_specs=pl.BlockSpec((FFN_TILE, d), lambda i: (i, 0)),
        out_shape=jax.ShapeDtypeStruct((n_tok, d), F32),
        scratch_shapes=[
            pltpu.VMEM(w_in_shape, BF16),
            pltpu.VMEM(w_out_shape, BF16),
            pltpu.VMEM((2, w_in_shape[0] // W_STAGE_CHUNKS, w_in_shape[1]), F32),
            pltpu.VMEM((2, w_out_shape[0] // W_STAGE_CHUNKS, w_out_shape[1]), F32),
            pltpu.SemaphoreType.DMA((2,)),
            pltpu.SemaphoreType.DMA((2,)),
        ],
        input_output_aliases={0: 0} if in_place else {},
        compiler_params=pltpu.CompilerParams(
            dimension_semantics=("arbitrary",), vmem_limit_bytes=VMEM_LIMIT_BYTES),
        name="ffn",
    )(x2d, mod, fmod, g_all, g_final, w_in_all, w_out_all)


def _conv_group(z, ubuf_ref, ushift_ref, wdw_ref, b_dw, g_ln, b_ln):
    tile = z.shape[0]
    ubuf_ref[CONV_PAD:CONV_PAD + tile, :] = z[:, :D_CONV] * _sigmoid(z[:, D_CONV:2 * D_CONV])
    first = CONV_PAD - (CONV_WIDTH - 1)
    shifted_rows = ushift_ref.shape[1]
    for s in range(1, SUBLANES):
        ushift_ref[s - 1] = ubuf_ref[s:s + shifted_rows, :]
    blocks = []
    for r0 in range(0, tile, CONV_ROWS):
        acc = jnp.zeros((CONV_ROWS, D_CONV), F32)
        for k in range(CONV_WIDTH):
            base, s = (first + k) // SUBLANES * SUBLANES, (first + k) % SUBLANES
            rows = slice(r0 + base, r0 + base + CONV_ROWS)
            tap = ubuf_ref[rows, :] if s == 0 else ushift_ref[s - 1, rows, :]
            acc = acc + wdw_ref[k:k + 1, :] * tap
        blocks.append(acc)
    y = jnp.concatenate(blocks, axis=0) + b_dw
    ubuf_ref[0:CONV_PAD, :] = ubuf_ref[tile:tile + CONV_PAD, :]
    yc = y - jnp.mean(y, axis=-1, keepdims=True)
    yn = yc * lax.rsqrt(jnp.mean(yc * yc, axis=-1, keepdims=True) + EPS)
    return _silu(yn * g_ln + b_ln)


def _gla_subtile(q, k, v, log_a, s_ref, gnorm_ref):
    t = GLA_TILE
    row = lax.broadcasted_iota(jnp.int32, (t, t), 0)
    col = lax.broadcasted_iota(jnp.int32, (t, t), 1)
    lower = col <= row
    tri = jnp.where(lower, 1.0, 0.0).astype(BF16)
    ones = jnp.ones((t, LANES), BF16)
    parts = _split_bf16(log_a, 2)
    cum = sum(_dot(tri, p) for p in parts)
    total_cols = sum(_dot_tn(p, ones) for p in parts)
    mid = cum[CHUNK - 1:CHUNK, :]
    end = cum[t - 1:t, :]
    rel = cum - mid
    e_pos = jnp.exp(rel)
    e_neg = jnp.exp(-rel)
    qs = q * (GLA_DK ** -0.5)
    q_fwd = (qs * e_pos).astype(BF16)
    q_bwd = (qs * e_neg).astype(BF16)
    q_in = (qs * jnp.exp(cum)).astype(BF16)
    k_fwd = (k * e_pos).astype(BF16)
    k_bwd = (k * e_neg).astype(BF16)
    k_out = (k * jnp.exp(end - cum)).astype(BF16)
    v16 = v.astype(BF16)

    same_chunk = (row >= CHUNK) == (col >= CHUNK)
    lane_head = lax.broadcasted_iota(jnp.int32, (t, D_QK), 1) // GLA_DK
    zero16 = jnp.zeros((), BF16)

    def per_head_keys(k16):
        return jnp.concatenate([jnp.where(lane_head == h, k16, zero16) for h in range(GLA_HEADS)], axis=0)

    att_fwd = _dot_nt(q_fwd, per_head_keys(k_bwd))
    att_bwd = _dot_nt(q_bwd, per_head_keys(k_fwd))
    s_old = s_ref[...]
    o_state = _dot(q_in, s_old.astype(BF16))
    outs = []
    for h in range(GLA_HEADS):
        cols = slice(h * t, (h + 1) * t)
        vh = v16[:, h * GLA_DV:(h + 1) * GLA_DV]
        att = jnp.where(lower, att_fwd[:, cols], jnp.where(same_chunk, att_bwd[:, cols], 0.0))
        o = _dot(att.astype(BF16), vh) + o_state[:, h * GLA_DV:(h + 1) * GLA_DV]
        o = o * lax.rsqrt(jnp.mean(o * o, axis=-1, keepdims=True) + EPS) * gnorm_ref[h:h + 1, :]
        outs.append(o)
    upd = _dot_tn(k_out, v16)
    decay = jnp.exp(total_cols)
    for h in range(GLA_HEADS):
        rows, cols = slice(h * GLA_DK, (h + 1) * GLA_DK), slice(h * GLA_DV, (h + 1) * GLA_DV)
        s_ref[rows, cols] = decay[rows, :] * s_old[rows, cols] + upd[rows, cols]
    return jnp.concatenate(outs, axis=-1)


def _mixer_kernel(x_ref, mod_ref, g_ref, w_in_ref, wgu_ref, bg_ref, wdw_ref, bdw_ref,
                  gln_ref, bln_ref, gnorm_ref, w_out_ref, o_ref, s_ref, ubuf_ref, ushift_ref, mix_ref, *, layer):
    this = slice(layer, layer + 1)
    @pl.when(pl.program_id(1) == 0)
    def _():
        s_ref[...] = jnp.zeros_like(s_ref)
        ubuf_ref[0:CONV_PAD, :] = jnp.zeros((CONV_PAD, D_CONV), F32)

    for m0 in range(0, MIX_TILE, MIX_SUB):
        sub = slice(m0, m0 + MIX_SUB)
        x = x_ref[sub, :]
        tile = MIX_SUB
        h = _rms_mod(x, g_ref[this, :], mod_ref[3:4, :], mod_ref[4:5, :]).astype(BF16)
        z = _dot(h, w_in_ref[:, :D_MAIN])
        glr = _dot(h, w_in_ref[:, D_MAIN:])
        y_conv = _dot(_conv_group(z, ubuf_ref, ushift_ref, wdw_ref, bdw_ref[this, :], gln_ref[this, :],
                                  bln_ref[this, :]).astype(BF16),
                      w_out_ref[:D_CONV, :])

        log_a = _log_sigmoid(_dot(glr.astype(BF16), wgu_ref[...]) + bg_ref[this, :]) * (1.0 / GATE_TAU)
        q0, k0, v0, r0 = 2 * D_CONV, 2 * D_CONV + D_QK, 2 * D_CONV + 2 * D_QK, 2 * D_CONV + 2 * D_QK + D_GLA
        for t0 in range(0, tile, GLA_TILE):
            rows = slice(t0, t0 + GLA_TILE)
            o = _gla_subtile(z[rows, q0:k0], z[rows, k0:v0], z[rows, v0:r0], log_a[rows, :], s_ref, gnorm_ref)
            mix_ref[rows, :] = (o * _silu(z[rows, r0:])).astype(BF16)

        y = y_conv + _dot(mix_ref[...], w_out_ref[D_CONV:, :])
        o_ref[sub, :] = x + mod_ref[5:6, :] * y


def _mixer_call(x, mod, g, w_in16, wgu, bg, wdw, bdw, gln, bln, gnorm, w_out16, layer):
    bsz, seq, d = x.shape
    resident = pl.Buffered(1)

    def whole(a):
        return pl.BlockSpec(a.shape, lambda b, t: (0,) * a.ndim, pipeline_mode=resident)

    def of_layer(a):
        return pl.BlockSpec((None,) + a.shape[1:], lambda b, t: (layer, 0, 0), pipeline_mode=resident)

    return pl.pallas_call(
        functools.partial(_mixer_kernel, layer=layer),
        grid=(bsz, seq // MIX_TILE),
        in_specs=[
            pl.BlockSpec((None, MIX_TILE, d), lambda b, t: (b, t, 0)),
            pl.BlockSpec((None, None, N_MOD, d), lambda b, t: (layer, b, 0, 0)),
            whole(g), of_layer(w_in16), of_layer(wgu), whole(bg), of_layer(wdw), whole(bdw),
            whole(gln), whole(bln), of_layer(gnorm), of_layer(w_out16),
        ],
        out_specs=pl.BlockSpec((None, MIX_TILE, d), lambda b, t: (b, t, 0)),
        out_shape=jax.ShapeDtypeStruct((bsz, seq, d), F32),
        scratch_shapes=[
            pltpu.VMEM((D_QK, D_GLA), F32),
            pltpu.VMEM((CONV_PAD + MIX_SUB, D_CONV), F32),
            pltpu.VMEM((SUBLANES - 1, MIX_SUB + CONV_PAD - SUBLANES, D_CONV), F32),
            pltpu.VMEM((MIX_SUB, D_GLA), BF16),
        ],
        input_output_aliases={0: 0},
        compiler_params=pltpu.CompilerParams(
            dimension_semantics=("arbitrary", "arbitrary"), vmem_limit_bytes=VMEM_LIMIT_BYTES),
        name="mixer",
    )(x, mod, g, w_in16, wgu, bg, wdw, bdw, gln, bln, gnorm, w_out16)


def kernel(x, c, w_ada, b_ada, g_norm_ffn1, w_ffn1_in, w_ffn1_out, g_norm_mix, w_in, w_dw, b_dw, g_conv_ln, b_conv_ln, w_gate_up, b_gate, g_gla_norm, w_out, g_norm_ffn2, w_ffn2_in, w_ffn2_out, g_norm_final, w_ada_final, b_ada_final):
    bsz, seq, d = x.shape
    depth = w_ada.shape[0]
    assert seq % MIX_TILE == 0 and seq % FFN_TILE == 0 and FFN_TILE % FFN_SUB == 0 and MIX_TILE % MIX_SUB == 0
    assert MIX_SUB % GLA_TILE == 0 and MIX_SUB % CONV_ROWS == 0
    assert bsz <= SUBLANES
    tiles_per_batch = seq // FFN_TILE

    c_pad = jnp.pad(c, ((0, SUBLANES - bsz), (0, 0)))
    mod = _ada_call(c_pad, w_ada, b_ada[:, None, :], ADA_COLS)[:, :bsz].reshape(depth, bsz, N_MOD, d)
    fmod = _ada_call(c_pad, w_ada_final[None], b_ada_final[None, None, :], 2 * d)[0, :bsz].reshape(bsz, 2, d)

    w_in16 = w_in.astype(BF16)
    w_out16 = w_out.astype(BF16)
    wgu = w_gate_up.astype(BF16)

    x2d = x.reshape(bsz * seq, d)
    g_final = g_norm_final[None]
    for l in range(depth):
        x2d = _ffn_call(x2d, mod, fmod, g_norm_ffn1, g_final, w_ffn1_in, w_ffn1_out, l, 0, tiles_per_batch, False, l > 0)
        x3d = _mixer_call(x2d.reshape(bsz, seq, d), mod, g_norm_mix, w_in16, wgu, b_gate, w_dw, b_dw, g_conv_ln,
                          b_conv_ln, g_gla_norm, w_out16, l)
        x2d = _ffn_call(x3d.reshape(bsz * seq, d), mod, fmod, g_norm_ffn2, g_final, w_ffn2_in, w_ffn2_out, l, 6,
                        tiles_per_batch, l == depth - 1, True)
    return x2d.reshape(bsz, seq, d)
```

```python
import functools

import jax
import jax.numpy as jnp
from jax import lax
from jax.experimental import pallas as pl
from jax.experimental.pallas import tpu as pltpu

D_MODEL = 1024
D_FF = 2816
D_CONV = 512
CONV_WIDTH = 31
GLA_HEADS = 4
GLA_DK = 64
GLA_DV = 128
D_QK = GLA_HEADS * GLA_DK
D_GLA = GLA_HEADS * GLA_DV
D_MIX = D_CONV + D_GLA
GATE_RANK = 16
GATE_TAU = 16.0
CHUNK = 64
N_MOD = 9
EPS = 1e-6
D_MAIN = 2 * D_CONV + 2 * D_QK + 2 * D_GLA

LANES = 128
SUBLANES = 8
VMEM_LIMIT_BYTES = 56 * 1024 * 1024

FFN_TILE = 1024
FFN_SUB = 128
MIX_TILE = 512
MIX_SUB = 256
GLA_TILE = 2 * CHUNK
CONV_ROWS = 32
CONV_PAD = 32
W_STAGE_CHUNKS = 4

F32 = jnp.float32
BF16 = jnp.bfloat16


def _dot(a, b):
    return jnp.dot(a, b, preferred_element_type=F32)


def _dot_nt(a, b):
    return lax.dot_general(a, b, (((1,), (1,)), ((), ())), preferred_element_type=F32)


def _dot_tn(a, b):
    return lax.dot_general(a, b, (((0,), (0,)), ((), ())), preferred_element_type=F32)


def _sigmoid(x):
    return 1.0 / (1.0 + jnp.exp(-x))


def _silu(x):
    return x * _sigmoid(x)


def _log_sigmoid(x):
    return jnp.minimum(x, 0.0) - jnp.log(1.0 + jnp.exp(-jnp.abs(x)))


def _rms_mod(x, g, shift, scale):
    y = x * lax.rsqrt(jnp.mean(x * x, axis=-1, keepdims=True) + EPS)
    return (y * g) * (1.0 + scale) + shift


def _split_bf16(x, parts):
    out = []
    for _ in range(parts):
        p = x.astype(BF16)
        out.append(p)
        x = x - p.astype(F32)
    return out


def _ada_kernel(c_ref, w_ref, b_ref, o_ref):
    c = _silu(c_ref[...])
    o_ref[...] = _dot(c.astype(BF16), w_ref[...].astype(BF16)) + b_ref[...]


def _ada_call(c_pad, w, b, tn):
    n_layers, d, n = w.shape
    return pl.pallas_call(
        _ada_kernel,
        grid=(n_layers, n // tn),
        in_specs=[
            pl.BlockSpec((SUBLANES, d), lambda l, j: (0, 0)),
            pl.BlockSpec((None, d, tn), lambda l, j: (l, 0, j)),
            pl.BlockSpec((None, 1, tn), lambda l, j: (l, 0, j)),
        ],
        out_specs=pl.BlockSpec((None, SUBLANES, tn), lambda l, j: (l, 0, j)),
        out_shape=jax.ShapeDtypeStruct((n_layers, SUBLANES, n), F32),
        compiler_params=pltpu.CompilerParams(
            dimension_semantics=("arbitrary", "arbitrary"), vmem_limit_bytes=VMEM_LIMIT_BYTES),
        name="ada",
    )(c_pad, w, b)


def _load_as_bf16(src_hbm_ref, dst_ref, stage_ref, sem_ref):
    rows = stage_ref.shape[1]
    n_chunks = src_hbm_ref.shape[0] // rows
    assert n_chunks * rows == src_hbm_ref.shape[0] == dst_ref.shape[0]

    def chunk_copy(c):
        return pltpu.make_async_copy(src_hbm_ref.at[pl.ds(c * rows, rows), :], stage_ref.at[c % 2], sem_ref.at[c % 2])

    chunk_copy(0).start()
    for c in range(n_chunks):
        if c + 1 < n_chunks:
            chunk_copy(c + 1).start()
        chunk_copy(c).wait()
        dst_ref[c * rows:(c + 1) * rows, :] = stage_ref[c % 2].astype(BF16)


def _ffn_kernel(x_ref, mod_ref, fmod_ref, g_ref, g_final_ref, w_in_hbm_ref, w_out_hbm_ref, o_ref,
                w_in_ref, w_out_ref, stage_in_ref, stage_out_ref, sem_in_ref, sem_out_ref, *, layer, mod_row, final):
    @pl.when(pl.program_id(0) == 0)
    def _():
        _load_as_bf16(w_in_hbm_ref.at[layer], w_in_ref, stage_in_ref, sem_in_ref)
        _load_as_bf16(w_out_hbm_ref.at[layer], w_out_ref, stage_out_ref, sem_out_ref)

    shift = mod_ref[mod_row:mod_row + 1, :]
    scale = mod_ref[mod_row + 1:mod_row + 2, :]
    gate = mod_ref[mod_row + 2:mod_row + 3, :]
    for m0 in range(0, FFN_TILE, FFN_SUB):
        rows = slice(m0, m0 + FFN_SUB)
        x = x_ref[rows, :]
        h = _rms_mod(x, g_ref[layer:layer + 1, :], shift, scale).astype(BF16)
        gu = _dot(h, w_in_ref[...])
        act = (_silu(gu[:, :D_FF]) * gu[:, D_FF:]).astype(BF16)
        y = x + (0.5 * gate) * _dot(act, w_out_ref[...])
        if final:
            y = _rms_mod(y, g_final_ref[...], fmod_ref[0:1, :], fmod_ref[1:2, :])
        o_ref[rows, :] = y


def _ffn_call(x2d, mod, fmod, g_all, g_final, w_in_all, w_out_all, layer, mod_row, tiles_per_batch, final, in_place):
    n_tok, d = x2d.shape
    w_in_shape, w_out_shape = w_in_all.shape[1:], w_out_all.shape[1:]
    assert w_in_shape[0] % W_STAGE_CHUNKS == 0 and w_out_shape[0] % W_STAGE_CHUNKS == 0
    return pl.pallas_call(
        functools.partial(_ffn_kernel, layer=layer, mod_row=mod_row, final=final),
        grid=(n_tok // FFN_TILE,),
        in_specs=[
            pl.BlockSpec((FFN_TILE, d), lambda i: (i, 0)),
            pl.BlockSpec((None, None, N_MOD, d), lambda i: (layer, i // tiles_per_batch, 0, 0)),
            pl.BlockSpec((None, 2, d), lambda i: (i // tiles_per_batch, 0, 0)),
            pl.BlockSpec(g_all.shape, lambda i: (0, 0)),
            pl.BlockSpec((1, d), lambda i: (0, 0)),
            pl.BlockSpec(memory_space=pl.ANY),
            pl.BlockSpec(memory_space=pl.ANY),
        ],
        out_specs=pl.BlockSpec((FFN_TILE, d), lambda i: (i, 0)),
        out_shape=jax.ShapeDtypeStruct((n_tok, d), F32),
        scratch_shapes=[
            pltpu.VMEM(w_in_shape, BF16),
            pltpu.VMEM(w_out_shape, BF16),
            pltpu.VMEM((2, w_in_shape[0] // W_STAGE_CHUNKS, w_in_shape[1]), F32),
            pltpu.VMEM((2, w_out_shape[0] // W_STAGE_CHUNKS, w_out_shape[1]), F32),
            pltpu.SemaphoreType.DMA((2,)),
            pltpu.SemaphoreType.DMA((2,)),
        ],
        input_output_aliases={0: 0} if in_place else {},
        compiler_params=pltpu.CompilerParams(
            dimension_semantics=("arbitrary",), vmem_limit_bytes=VMEM_LIMIT_BYTES),
        name="ffn",
    )(x2d, mod, fmod, g_all, g_final, w_in_all, w_out_all)


def _conv_group(z, ubuf_ref, ushift_ref, wdw_ref, b_dw, g_ln, b_ln):
    tile = z.shape[0]
    ubuf_ref[CONV_PAD:CONV_PAD + tile, :] = z[:, :D_CONV] * _sigmoid(z[:, D_CONV:2 * D_CONV])
    first = CONV_PAD - (CONV_WIDTH - 1)
    shifted_rows = ushift_ref.shape[1]
    for s in range(1, SUBLANES):
        ushift_ref[s - 1] = ubuf_ref[s:s + shifted_rows, :]
    blocks = []
    for r0 in range(0, tile, CONV_ROWS):
        acc = jnp.zeros((CONV_ROWS, D_CONV), F32)
        for k in range(CONV_WIDTH):
            base, s = (first + k) // SUBLANES * SUBLANES, (first + k) % SUBLANES
            rows = slice(r0 + base, r0 + base + CONV_ROWS)
            tap = ubuf_ref[rows, :] if s == 0 else ushift_ref[s - 1, rows, :]
            acc = acc + wdw_ref[k:k + 1, :] * tap
        blocks.append(acc)
    y = jnp.concatenate(blocks, axis=0) + b_dw
    ubuf_ref[0:CONV_PAD, :] = ubuf_ref[tile:tile + CONV_PAD, :]
    yc = y - jnp.mean(y, axis=-1, keepdims=True)
    yn = yc * lax.rsqrt(jnp.mean(yc * yc, axis=-1, keepdims=True) + EPS)
    return _silu(yn * g_ln + b_ln)


def _gla_subtile(q, k, v, log_a, s_ref, gnorm_ref):
    t = GLA_TILE
    row = lax.broadcasted_iota(jnp.int32, (t, t), 0)
    col = lax.broadcasted_iota(jnp.int32, (t, t), 1)
    lower = col <= row
    tri = jnp.where(lower, 1.0, 0.0).astype(BF16)
    ones = jnp.ones((t, LANES), BF16)
    parts = _split_bf16(log_a, 2)
    cum = sum(_dot(tri, p) for p in parts)
    total_cols = sum(_dot_tn(p, ones) for p in parts)
    mid = cum[CHUNK - 1:CHUNK, :]
    end = cum[t - 1:t, :]
    rel = cum - mid
    e_pos = jnp.exp(rel)
    e_neg = jnp.exp(-rel)
    qs = q * (GLA_DK ** -0.5)
    q_fwd = (qs * e_pos).astype(BF16)
    q_bwd = (qs * e_neg).astype(BF16)
    q_in = (qs * jnp.exp(cum)).astype(BF16)
    k_fwd = (k * e_pos).astype(BF16)
    k_bwd = (k * e_neg).astype(BF16)
    k_out = (k * jnp.exp(end - cum)).astype(BF16)
    v16 = v.astype(BF16)

    same_chunk = (row >= CHUNK) == (col >= CHUNK)
    lane_head = lax.broadcasted_iota(jnp.int32, (t, D_QK), 1) // GLA_DK
    zero16 = jnp.zeros((), BF16)

    def per_head_keys(k16):
        return jnp.concatenate([jnp.where(lane_head == h, k16, zero16) for h in range(GLA_HEADS)], axis=0)

    att_fwd = _dot_nt(q_fwd, per_head_keys(k_bwd))
    att_bwd = _dot_nt(q_bwd, per_head_keys(k_fwd))
    s_old = s_ref[...]
    o_state = _dot(q_in, s_old.astype(BF16))
    outs = []
    for h in range(GLA_HEADS):
        cols = slice(h * t, (h + 1) * t)
        vh = v16[:, h * GLA_DV:(h + 1) * GLA_DV]
        att = jnp.where(lower, att_fwd[:, cols], jnp.where(same_chunk, att_bwd[:, cols], 0.0))
        o = _dot(att.astype(BF16), vh) + o_state[:, h * GLA_DV:(h + 1) * GLA_DV]
        o = o * lax.rsqrt(jnp.mean(o * o, axis=-1, keepdims=True) + EPS) * gnorm_ref[h:h + 1, :]
        outs.append(o)
    upd = _dot_tn(k_out, v16)
    decay = jnp.exp(total_cols)
    for h in range(GLA_HEADS):
        rows, cols = slice(h * GLA_DK, (h + 1) * GLA_DK), slice(h * GLA_DV, (h + 1) * GLA_DV)
        s_ref[rows, cols] = decay[rows, :] * s_old[rows, cols] + upd[rows, cols]
    return jnp.concatenate(outs, axis=-1)


def _mixer_kernel(x_ref, mod_ref, g_ref, w_in_ref, wgu_ref, bg_ref, wdw_ref, bdw_ref,
                  gln_ref, bln_ref, gnorm_ref, w_out_ref, o_ref, s_ref, ubuf_ref, ushift_ref, mix_ref, *, layer):
    this = slice(layer, layer + 1)
    @pl.when(pl.program_id(1) == 0)
    def _():
        s_ref[...] = jnp.zeros_like(s_ref)
        ubuf_ref[0:CONV_PAD, :] = jnp.zeros((CONV_PAD, D_CONV), F32)

    for m0 in range(0, MIX_TILE, MIX_SUB):
        sub = slice(m0, m0 + MIX_SUB)
        x = x_ref[sub, :]
        tile = MIX_SUB
        h = _rms_mod(x, g_ref[this, :], mod_ref[3:4, :], mod_ref[4:5, :]).astype(BF16)
        z = _dot(h, w_in_ref[:, :D_MAIN])
        glr = _dot(h, w_in_ref[:, D_MAIN:])
        y_conv = _dot(_conv_group(z, ubuf_ref, ushift_ref, wdw_ref, bdw_ref[this, :], gln_ref[this, :],
                                  bln_ref[this, :]).astype(BF16),
                      w_out_ref[:D_CONV, :])

        log_a = _log_sigmoid(_dot(glr.astype(BF16), wgu_ref[...]) + bg_ref[this, :]) * (1.0 / GATE_TAU)
        q0, k0, v0, r0 = 2 * D_CONV, 2 * D_CONV + D_QK, 2 * D_CONV + 2 * D_QK, 2 * D_CONV + 2 * D_QK + D_GLA
        for t0 in range(0, tile, GLA_TILE):
            rows = slice(t0, t0 + GLA_TILE)
            o = _gla_subtile(z[rows, q0:k0], z[rows, k0:v0], z[rows, v0:r0], log_a[rows, :], s_ref, gnorm_ref)
            mix_ref[rows, :] = (o * _silu(z[rows, r0:])).astype(BF16)

        y = y_conv + _dot(mix_ref[...], w_out_ref[D_CONV:, :])
        o_ref[sub, :] = x + mod_ref[5:6, :] * y


def _mixer_call(x, mod, g, w_in16, wgu, bg, wdw, bdw, gln, bln, gnorm, w_out16, layer):
    bsz, seq, d = x.shape
    resident = pl.Buffered(1)

    def whole(a):
        return pl.BlockSpec(a.shape, lambda b, t: (0,) * a.ndim, pipeline_mode=resident)

    def of_layer(a):
        return pl.BlockSpec((None,) + a.shape[1:], lambda b, t: (layer, 0, 0), pipeline_mode=resident)

    return pl.pallas_call(
        functools.partial(_mixer_kernel, layer=layer),
        grid=(bsz, seq // MIX_TILE),
        in_specs=[
            pl.BlockSpec((None, MIX_TILE, d), lambda b, t: (b, t, 0)),
            pl.BlockSpec((None, None, N_MOD, d), lambda b, t: (layer, b, 0, 0)),
            whole(g), of_layer(w_in16), of_layer(wgu), whole(bg), of_layer(wdw), whole(bdw),
            whole(gln), whole(bln), of_layer(gnorm), of_layer(w_out16),
        ],
        out_specs=pl.BlockSpec((None, MIX_TILE, d), lambda b, t: (b, t, 0)),
        out_shape=jax.ShapeDtypeStruct((bsz, seq, d), F32),
        scratch_shapes=[
            pltpu.VMEM((D_QK, D_GLA), F32),
            pltpu.VMEM((CONV_PAD + MIX_SUB, D_CONV), F32),
            pltpu.VMEM((SUBLANES - 1, MIX_SUB + CONV_PAD - SUBLANES, D_CONV), F32),
            pltpu.VMEM((MIX_SUB, D_GLA), BF16),
        ],
        input_output_aliases={0: 0},
        compiler_params=pltpu.CompilerParams(
            dimension_semantics=("arbitrary", "arbitrary"), vmem_limit_bytes=VMEM_LIMIT_BYTES),
        name="mixer",
    )(x, mod, g, w_in16, wgu, bg, wdw, bdw, gln, bln, gnorm, w_out16)


def kernel(x, c, w_ada, b_ada, g_norm_ffn1, w_ffn1_in, w_ffn1_out, g_norm_mix, w_in, w_dw, b_dw, g_conv_ln, b_conv_ln, w_gate_up, b_gate, g_gla_norm, w_out, g_norm_ffn2, w_ffn2_in, w_ffn2_out, g_norm_final, w_ada_final, b_ada_final):
    bsz, seq, d = x.shape
    depth = w_ada.shape[0]
    assert seq % MIX_TILE == 0 and seq % FFN_TILE == 0 and FFN_TILE % FFN_SUB == 0 and MIX_TILE % MIX_SUB == 0
    assert MIX_SUB % GLA_TILE == 0 and MIX_SUB % CONV_ROWS == 0
    assert bsz <= SUBLANES
    tiles_per_batch = seq // FFN_TILE

    c_pad = jnp.pad(c, ((0, SUBLANES - bsz), (0, 0)))
    mod = _ada_call(c_pad, w_ada, b_ada[:, None, :], 1024)[:, :bsz].reshape(depth, bsz, N_MOD, d)
    fmod = _ada_call(c_pad, w_ada_final[None], b_ada_final[None, None, :], 1024)[0, :bsz].reshape(bsz, 2, d)

    w_in16 = w_in.astype(BF16)
    w_out16 = w_out.astype(BF16)
    wgu = w_gate_up.astype(BF16)

    x2d = x.reshape(bsz * seq, d)
    g_final = g_norm_final[None]
    for l in range(depth):
        x2d = _ffn_call(x2d, mod, fmod, g_norm_ffn1, g_final, w_ffn1_in, w_ffn1_out, l, 0, tiles_per_batch, False, l > 0)
        x3d = _mixer_call(x2d.reshape(bsz, seq, d), mod, g_norm_mix, w_in16, wgu, b_gate, w_dw, b_dw, g_conv_ln,
                          b_conv_ln, g_gla_norm, w_out16, l)
        x2d = _ffn_call(x3d.reshape(bsz * seq, d), mod, fmod, g_norm_ffn2, g_final, w_ffn2_in, w_ffn2_out, l, 6,
                        tiles_per_batch, l == depth - 1, True)
    return x2d.reshape(bsz, seq, d)
```

```python
import functools

import jax
import jax.numpy as jnp
from jax import lax
from jax.experimental import pallas as pl
from jax.experimental.pallas import tpu as pltpu

D_MODEL = 1024
D_FF = 2816
D_CONV = 512
CONV_WIDTH = 31
GLA_HEADS = 4
GLA_DK = 64
GLA_DV = 128
D_QK = GLA_HEADS * GLA_DK
D_GLA = GLA_HEADS * GLA_DV
D_MIX = D_CONV + D_GLA
GATE_RANK = 16
GATE_TAU = 16.0
CHUNK = 64
N_MOD = 9
EPS = 1e-6
D_MAIN = 2 * D_CONV + 2 * D_QK + 2 * D_GLA

LANES = 128
SUBLANES = 8
VMEM_LIMIT_BYTES = 56 * 1024 * 1024

FFN_TILE = 1024
FFN_SUB = 128
MIX_TILE = 512
MIX_SUB = 256
GLA_TILE = 2 * CHUNK
CONV_ROWS = 32
CONV_PAD = 32
W_STAGE_CHUNKS = 4

F32 = jnp.float32
BF16 = jnp.bfloat16


def _dot(a, b):
    return jnp.dot(a, b, preferred_element_type=F32)


def _dot_nt(a, b):
    return lax.dot_general(a, b, (((1,), (1,)), ((), ())), preferred_element_type=F32)


def _dot_tn(a, b):
    return lax.dot_general(a, b, (((0,), (0,)), ((), ())), preferred_element_type=F32)


def _sigmoid(x):
    return 1.0 / (1.0 + jnp.exp(-x))


def _silu(x):
    return x * _sigmoid(x)


def _log_sigmoid(x):
    return jnp.minimum(x, 0.0) - jnp.log(1.0 + jnp.exp(-jnp.abs(x)))


def _rms_mod(x, g, shift, scale):
    y = x * lax.rsqrt(jnp.mean(x * x, axis=-1, keepdims=True) + EPS)
    return (y * g) * (1.0 + scale) + shift


def _split_bf16(x, parts):
    out = []
    for _ in range(parts):
        p = x.astype(BF16)
        out.append(p)
        x = x - p.astype(F32)
    return out


def _ada_kernel(c_ref, w_ref, b_ref, o_ref):
    c = _silu(c_ref[...])
    o_ref[...] = _dot(c.astype(BF16), w_ref[...].astype(BF16)) + b_ref[...]


def _ada_call(c_pad, w, b, tn):
    n_layers, d, n = w.shape
    return pl.pallas_call(
        _ada_kernel,
        grid=(n_layers, n // tn),
        in_specs=[
            pl.BlockSpec((SUBLANES, d), lambda l, j: (0, 0)),
            pl.BlockSpec((None, d, tn), lambda l, j: (l, 0, j)),
            pl.BlockSpec((None, 1, tn), lambda l, j: (l, 0, j)),
        ],
        out_specs=pl.BlockSpec((None, SUBLANES, tn), lambda l, j: (l, 0, j)),
        out_shape=jax.ShapeDtypeStruct((n_layers, SUBLANES, n), F32),
        compiler_params=pltpu.CompilerParams(
            dimension_semantics=("arbitrary", "arbitrary"), vmem_limit_bytes=VMEM_LIMIT_BYTES),
        name="ada",
    )(c_pad, w, b)


def _load_as_bf16(src_hbm_ref, dst_ref, stage_ref, sem_ref):
    rows = stage_ref.shape[1]
    n_chunks = src_hbm_ref.shape[0] // rows
    assert n_chunks * rows == src_hbm_ref.shape[0] == dst_ref.shape[0]

    def chunk_copy(c):
        return pltpu.make_async_copy(src_hbm_ref.at[pl.ds(c * rows, rows), :], stage_ref.at[c % 2], sem_ref.at[c % 2])

    chunk_copy(0).start()
    for c in range(n_chunks):
        if c + 1 < n_chunks:
            chunk_copy(c + 1).start()
        chunk_copy(c).wait()
        dst_ref[c * rows:(c + 1) * rows, :] = stage_ref[c % 2].astype(BF16)


def _ffn_kernel(x_ref, mod_ref, fmod_ref, g_ref, g_final_ref, w_in_hbm_ref, w_out_hbm_ref, o_ref,
                w_in_ref, w_out_ref, stage_in_ref, stage_out_ref, sem_in_ref, sem_out_ref, *, layer, mod_row, final):
    @pl.when(pl.program_id(0) == 0)
    def _():
        _load_as_bf16(w_in_hbm_ref.at[layer], w_in_ref, stage_in_ref, sem_in_ref)
        _load_as_bf16(w_out_hbm_ref.at[layer], w_out_ref, stage_out_ref, sem_out_ref)

    shift = mod_ref[mod_row:mod_row + 1, :]
    scale = mod_ref[mod_row + 1:mod_row + 2, :]
    gate = mod_ref[mod_row + 2:mod_row + 3, :]
    for m0 in range(0, FFN_TILE, FFN_SUB):
        rows = slice(m0, m0 + FFN_SUB)
        x = x_ref[rows, :]
        h = _rms_mod(x, g_ref[layer:layer + 1, :], shift, scale).astype(BF16)
        gu = _dot(h, w_in_ref[...])
        act = (_silu(gu[:, :D_FF]) * gu[:, D_FF:]).astype(BF16)
        y = x + (0.5 * gate) * _dot(act, w_out_ref[...])
        if final:
            y = _rms_mod(y, g_final_ref[...], fmod_ref[0:1, :], fmod_ref[1:2, :])
        o_ref[rows, :] = y


def _ffn_call(x2d, mod, fmod, g_all, g_final, w_in_all, w_out_all, layer, mod_row, tiles_per_batch, final, in_place):
    n_tok, d = x2d.shape
    w_in_shape, w_out_shape = w_in_all.shape[1:], w_out_all.shape[1:]
    assert w_in_shape[0] % W_STAGE_CHUNKS == 0 and w_out_shape[0] % W_STAGE_CHUNKS == 0
    return pl.pallas_call(
        functools.partial(_ffn_kernel, layer=layer, mod_row=mod_row, final=final),
        grid=(n_tok // FFN_TILE,),
        in_specs=[
            pl.BlockSpec((FFN_TILE, d), lambda i: (i, 0)),
            pl.BlockSpec((None, None, N_MOD, d), lambda i: (layer, i // tiles_per_batch, 0, 0)),
            pl.BlockSpec((None, 2, d), lambda i: (i // tiles_per_batch, 0, 0)),
            pl.BlockSpec(g_all.shape, lambda i: (0, 0)),
            pl.BlockSpec((1, d), lambda i: (0, 0)),
            pl.BlockSpec(memory_space=pl.ANY),
            pl.BlockSpec(memory_space=pl.ANY),
        ],
        out_specs=pl.BlockSpec((FFN_TILE, d), lambda i: (i, 0)),
        out_shape=jax.ShapeDtypeStruct((n_tok, d), F32),
        scratch_shapes=[
            pltpu.VMEM(w_in_shape, BF16),
            pltpu.VMEM(w_out_shape, BF16),
            pltpu.VMEM((2, w_in_shape[0] // W_STAGE_CHUNKS, w_in_shape[1]), F32),
            pltpu.VMEM((2, w_out_shape[0] // W_STAGE_CHUNKS, w_out_shape[1]), F32),
            pltpu.SemaphoreType.DMA((2,)),
            pltpu.SemaphoreType.DMA((2,)),
        ],
        input_output_aliases={0: 0} if in_place else {},
        compiler_params=pltpu.CompilerParams(
            dimension_semantics=("arbitrary",), vmem_limit_bytes=VMEM_LIMIT_BYTES),
        name="ffn",
    )(x2d, mod, fmod, g_all, g_final, w_in_all, w_out_all)


def _conv_group(z, ubuf_ref, ushift_ref, wdw_ref, b_dw, g_ln, b_ln):
    tile = z.shape[0]
    ubuf_ref[CONV_PAD:CONV_PAD + tile, :] = z[:, :D_CONV] * _sigmoid(z[:, D_CONV:2 * D_CONV])
    first = CONV_PAD - (CONV_WIDTH - 1)
    shifted_rows = ushift_ref.shape[1]
    for s in range(1, SUBLANES):
        ushift_ref[s - 1] = ubuf_ref[s:s + shifted_rows, :]
    blocks = []
    for r0 in range(0, tile, CONV_ROWS):
        acc = jnp.zeros((CONV_ROWS, D_CONV), F32)
        for k in range(CONV_WIDTH):
            base, s = (first + k) // SUBLANES * SUBLANES, (first + k) % SUBLANES
            rows = slice(r0 + base, r0 + base + CONV_ROWS)
            tap = ubuf_ref[rows, :] if s == 0 else ushift_ref[s - 1, rows, :]
            acc = acc + wdw_ref[k:k + 1, :] * tap
        blocks.append(acc)
    y = jnp.concatenate(blocks, axis=0) + b_dw
    ubuf_ref[0:CONV_PAD, :] = ubuf_ref[tile:tile + CONV_PAD, :]
    yc = y - jnp.mean(y, axis=-1, keepdims=True)
    yn = yc * lax.rsqrt(jnp.mean(yc * yc, axis=-1, keepdims=True) + EPS)
    return _silu(yn * g_ln + b_ln)


def _gla_subtile(q, k, v, log_a, s_ref, gnorm_ref):
    t = GLA_TILE
    row = lax.broadcasted_iota(jnp.int32, (t, t), 0)
    col = lax.broadcasted_iota(jnp.int32, (t, t), 1)
    lower = col <= row
    tri = jnp.where(lower, 1.0, 0.0).astype(BF16)
    ones = jnp.ones((t, LANES), BF16)
    parts = _split_bf16(log_a, 2)
    cum = sum(_dot(tri, p) for p in parts)
    total_cols = sum(_dot_tn(p, ones) for p in parts)
    mid = cum[CHUNK - 1:CHUNK, :]
    end = cum[t - 1:t, :]
    rel = cum - mid
    e_pos = jnp.exp(rel)
    e_neg = jnp.exp(-rel)
    qs = q * (GLA_DK ** -0.5)
    q_fwd = (qs * e_pos).astype(BF16)
    q_bwd = (qs * e_neg).astype(BF16)
    q_in = (qs * jnp.exp(cum)).astype(BF16)
    k_fwd = (k * e_pos).astype(BF16)
    k_bwd = (k * e_neg).astype(BF16)
    k_out = (k * jnp.exp(end - cum)).astype(BF16)
    v16 = v.astype(BF16)

    same_chunk = (row >= CHUNK) == (col >= CHUNK)
    lane_head = lax.broadcasted_iota(jnp.int32, (t, D_QK), 1) // GLA_DK
    zero16 = jnp.zeros((), BF16)

    def per_head_keys(k16):
        return jnp.concatenate([jnp.where(lane_head == h, k16, zero16) for h in range(GLA_HEADS)], axis=0)

    att_fwd = _dot_nt(q_fwd, per_head_keys(k_bwd))
    att_bwd = _dot_nt(q_bwd, per_head_keys(k_fwd))
    s_old = s_ref[...]
    o_state = _dot(q_in, s_old.astype(BF16))
    outs = []
    for h in range(GLA_HEADS):
        cols = slice(h * t, (h + 1) * t)
        vh = v16[:, h * GLA_DV:(h + 1) * GLA_DV]
        att = jnp.where(lower, att_fwd[:, cols], jnp.where(same_chunk, att_bwd[:, cols], 0.0))
        o = _dot(att.astype(BF16), vh) + o_state[:, h * GLA_DV:(h + 1) * GLA_DV]
        o = o * lax.rsqrt(jnp.mean(o * o, axis=-1, keepdims=True) + EPS) * gnorm_ref[h:h + 1, :]
        outs.append(o)
    upd = _dot_tn(k_out, v16)
    decay = jnp.exp(total_cols)
    for h in range(GLA_HEADS):
        rows, cols = slice(h * GLA_DK, (h + 1) * GLA_DK), slice(h * GLA_DV, (h + 1) * GLA_DV)
        s_ref[rows, cols] = decay[rows, :] * s_old[rows, cols] + upd[rows, cols]
    return jnp.concatenate(outs, axis=-1)


def _mixer_kernel(x_ref, mod_ref, g_ref, w_in_ref, wgu_ref, bg_ref, wdw_ref, bdw_ref,
                  gln_ref, bln_ref, gnorm_ref, w_out_ref, o_ref, s_ref, ubuf_ref, ushift_ref, mix_ref, z_ref, *, layer):
    this = slice(layer, layer + 1)
    @pl.when(pl.program_id(1) == 0)
    def _():
        s_ref[...] = jnp.zeros_like(s_ref)
        ubuf_ref[0:CONV_PAD, :] = jnp.zeros((CONV_PAD, D_CONV), F32)

    for m0 in range(0, MIX_TILE, MIX_SUB):
        sub = slice(m0, m0 + MIX_SUB)
        x = x_ref[sub, :]
        tile = MIX_SUB
        h = _rms_mod(x, g_ref[this, :], mod_ref[3:4, :], mod_ref[4:5, :]).astype(BF16)
        z_ref[...] = _dot(h, w_in_ref[:, :D_MAIN])
        z = z_ref
        glr = _dot(h, w_in_ref[:, D_MAIN:])
        y_conv = _dot(_conv_group(z, ubuf_ref, ushift_ref, wdw_ref, bdw_ref[this, :], gln_ref[this, :],
                                  bln_ref[this, :]).astype(BF16),
                      w_out_ref[:D_CONV, :])

        log_a = _log_sigmoid(_dot(glr.astype(BF16), wgu_ref[...]) + bg_ref[this, :]) * (1.0 / GATE_TAU)
        q0, k0, v0, r0 = 2 * D_CONV, 2 * D_CONV + D_QK, 2 * D_CONV + 2 * D_QK, 2 * D_CONV + 2 * D_QK + D_GLA
        for t0 in range(0, tile, GLA_TILE):
            rows = slice(t0, t0 + GLA_TILE)
            o = _gla_subtile(z[rows, q0:k0], z[rows, k0:v0], z[rows, v0:r0], log_a[rows, :], s_ref, gnorm_ref)
            mix_ref[rows, :] = (o * _silu(z[rows, r0:D_MAIN])).astype(BF16)

        y = y_conv + _dot(mix_ref[...], w_out_ref[D_CONV:, :])
        o_ref[sub, :] = x + mod_ref[5:6, :] * y


def _mixer_call(x, mod, g, w_in16, wgu, bg, wdw, bdw, gln, bln, gnorm, w_out16, layer):
    bsz, seq, d = x.shape
    resident = pl.Buffered(1)

    def whole(a):
        return pl.BlockSpec(a.shape, lambda b, t: (0,) * a.ndim, pipeline_mode=resident)

    def of_layer(a):
        return pl.BlockSpec((None,) + a.shape[1:], lambda b, t: (layer, 0, 0), pipeline_mode=resident)

    return pl.pallas_call(
        functools.partial(_mixer_kernel, layer=layer),
        grid=(bsz, seq // MIX_TILE),
        in_specs=[
            pl.BlockSpec((None, MIX_TILE, d), lambda b, t: (b, t, 0)),
            pl.BlockSpec((None, None, N_MOD, d), lambda b, t: (layer, b, 0, 0)),
            whole(g), of_layer(w_in16), of_layer(wgu), whole(bg), of_layer(wdw), whole(bdw),
            whole(gln), whole(bln), of_layer(gnorm), of_layer(w_out16),
        ],
        out_specs=pl.BlockSpec((None, MIX_TILE, d), lambda b, t: (b, t, 0)),
        out_shape=jax.ShapeDtypeStruct((bsz, seq, d), F32),
        scratch_shapes=[
            pltpu.VMEM((D_QK, D_GLA), F32),
            pltpu.VMEM((CONV_PAD + MIX_SUB, D_CONV), F32),
            pltpu.VMEM((SUBLANES - 1, MIX_SUB + CONV_PAD - SUBLANES, D_CONV), F32),
            pltpu.VMEM((MIX_SUB, D_GLA), BF16),
            pltpu.VMEM((MIX_SUB, D_MAIN), F32),
        ],
        input_output_aliases={0: 0},
        compiler_params=pltpu.CompilerParams(
            dimension_semantics=("arbitrary", "arbitrary"), vmem_limit_bytes=VMEM_LIMIT_BYTES),
        name="mixer",
    )(x, mod, g, w_in16, wgu, bg, wdw, bdw, gln, bln, gnorm, w_out16)


def kernel(x, c, w_ada, b_ada, g_norm_ffn1, w_ffn1_in, w_ffn1_out, g_norm_mix, w_in, w_dw, b_dw, g_conv_ln, b_conv_ln, w_gate_up, b_gate, g_gla_norm, w_out, g_norm_ffn2, w_ffn2_in, w_ffn2_out, g_norm_final, w_ada_final, b_ada_final):
    bsz, seq, d = x.shape
    depth = w_ada.shape[0]
    assert seq % MIX_TILE == 0 and seq % FFN_TILE == 0 and FFN_TILE % FFN_SUB == 0 and MIX_TILE % MIX_SUB == 0
    assert MIX_SUB % GLA_TILE == 0 and MIX_SUB % CONV_ROWS == 0
    assert bsz <= SUBLANES
    tiles_per_batch = seq // FFN_TILE

    c_pad = jnp.pad(c, ((0, SUBLANES - bsz), (0, 0)))
    mod = _ada_call(c_pad, w_ada, b_ada[:, None, :], 1024)[:, :bsz].reshape(depth, bsz, N_MOD, d)
    fmod = _ada_call(c_pad, w_ada_final[None], b_ada_final[None, None, :], 1024)[0, :bsz].reshape(bsz, 2, d)

    w_in16 = w_in.astype(BF16)
    w_out16 = w_out.astype(BF16)
    wgu = w_gate_up.astype(BF16)

    x2d = x.reshape(bsz * seq, d)
    g_final = g_norm_final[None]
    for l in range(depth):
        x2d = _ffn_call(x2d, mod, fmod, g_norm_ffn1, g_final, w_ffn1_in, w_ffn1_out, l, 0, tiles_per_batch, False, l > 0)
        x3d = _mixer_call(x2d.reshape(bsz, seq, d), mod, g_norm_mix, w_in16, wgu, b_gate, w_dw, b_dw, g_conv_ln,
                          b_conv_ln, g_gla_norm, w_out16, l)
        x2d = _ffn_call(x3d.reshape(bsz * seq, d), mod, fmod, g_norm_ffn2, g_final, w_ffn2_in, w_ffn2_out, l, 6,
                        tiles_per_batch, l == depth - 1, True)
    return x2d.reshape(bsz, seq, d)
```
